```python
import jax, jax.numpy as jnp
from jax import lax
import numpy as np

D_MODEL = 1024
BATCH = 8
SEQ = 4096
DEPTH = 1
DEC_BATCH = 32
DEC_SEQ = 64
PAST_LEN = 4096

CHUNK = 64
HEAD_DIM = 64
N_HEADS = D_MODEL // HEAD_DIM
RWKV_W = N_HEADS * HEAD_DIM
LORA_W = 64
LORA_A = 64
LORA_G = 128
SHIFT_W = 3 * RWKV_W + LORA_W + LORA_A + LORA_G
GMLP_W = D_MODEL
GMLP_CHUNK = 128
GMLP_GC = 128
GMLP_GROUPS = GMLP_W // GMLP_GC
IN_W = SHIFT_W + 2 * GMLP_W + 2 * D_MODEL
D_FF = -(-8 * D_MODEL // (3 * 256)) * 256
RMS_EPS = 1e-6
GN_EPS = 64e-5
LN_EPS = 1e-5

kernel_name = "rwkv7_gmlp_griffin_merge_stream_step"


def _rms_norm(x, g):
    xf = x.astype(jnp.float32)
    y = xf * lax.rsqrt(jnp.mean(xf * xf, axis=-1, keepdims=True) + RMS_EPS)
    return (y * g.astype(jnp.float32)).astype(x.dtype)


def _layer_norm(x, g, b):
    xf = x.astype(jnp.float32)
    mean = jnp.mean(xf, axis=-1, keepdims=True)
    var = jnp.mean(jnp.square(xf - mean), axis=-1, keepdims=True)
    return (xf - mean) * lax.rsqrt(var + LN_EPS) * g.astype(jnp.float32) + b.astype(jnp.float32)


def _wkv7_scan(S0, r, w, k, v, a, b):
    xs = tuple(jnp.swapaxes(t, 0, 1) for t in (r, w, k, v, a, b))

    def step(S, inp):
        r_t, w_t, k_t, v_t, a_t, b_t = inp
        sa = jnp.einsum('bhvk,bhk->bhv', S, a_t)
        S = S * w_t[:, :, None, :] + sa[..., None] * b_t[:, :, None, :] + v_t[..., None] * k_t[:, :, None, :]
        y = jnp.einsum('bhvk,bhk->bhv', S, r_t)
        return S, y

    S, ys = lax.scan(step, S0, xs)
    return jnp.swapaxes(ys, 0, 1), S


def _rwkv7(p, prev_row, S0, lp):
    B, T, _ = p.shape
    p_prev = jnp.concatenate([prev_row.astype(p.dtype), p[:, :-1]], axis=1)
    xm = p + (p_prev - p) * lp['mu_shift']
    o = 0
    r = xm[..., o:o + RWKV_W]; o += RWKV_W
    k = xm[..., o:o + RWKV_W]; o += RWKV_W
    v = xm[..., o:o + RWKV_W]; o += RWKV_W
    dw = xm[..., o:o + LORA_W]; o += LORA_W
    da = xm[..., o:o + LORA_A]; o += LORA_A
    dg = xm[..., o:o + LORA_G]
    f32 = lambda t: t.astype(jnp.float32)
    w_log = -jax.nn.softplus(-(f32(lp['w0']) + f32(jnp.tanh(dw) @ lp['w_lora_w']))) - 0.5
    decay = jnp.exp(-jnp.exp(w_log))
    a_rate = jax.nn.sigmoid(f32(lp['a0']) + f32(da @ lp['w_lora_a']))
    g = jax.nn.sigmoid(dg) @ lp['w_lora_g']
    hd = lambda t: f32(t).reshape(B, T, N_HEADS, HEAD_DIM)
    kk = hd(k * lp['k_k'])
    kk = kk / jnp.maximum(jnp.sqrt(jnp.sum(kk * kk, axis=-1, keepdims=True)), 1e-12)
    k_mod = f32(k) * (1.0 + (a_rate - 1.0) * f32(lp['k_a']))
    rh, kh, vh = hd(r), hd(k_mod), hd(v)
    ah = hd(a_rate)
    y, S = _wkv7_scan(S0.astype(jnp.float32), rh, hd(decay), kh, vh, -kk, kk * ah)
    mean = jnp.mean(y, axis=-1, keepdims=True)
    var = jnp.mean(jnp.square(y - mean), axis=-1, keepdims=True)
    yn = ((y - mean) * lax.rsqrt(var + GN_EPS)).reshape(B, T, RWKV_W) * f32(lp['gn_g']) + f32(lp['gn_b'])
    bonus = (jnp.sum(rh * kh * f32(lp['r_k']), axis=-1, keepdims=True) * vh).reshape(B, T, RWKV_W)
    out = ((yn + bonus) * f32(g)).astype(p.dtype)
    return out, S, p[:, -1:]


def _gmlp(u, vg, chunk, lp):
    B, T, _ = vg.shape
    vn = _layer_norm(vg, lp['ln_v_g'], lp['ln_v_b'])
    vc = vn.reshape(B, T // chunk, chunk, GMLP_GROUPS, GMLP_GC)
    mask = jnp.tril(jnp.ones((chunk, chunk), jnp.float32))
    ws = lp['w_spatial'][:, :chunk, :chunk].astype(jnp.float32) * mask
    bias = jnp.transpose(lp['b_spatial'][:, :chunk].astype(jnp.float32))
    z = jnp.einsum('gts,bnsgc->bntgc', ws, vc) + bias[None, None, :, :, None]
    out = (u.astype(jnp.float32) * z.reshape(B, T, GMLP_W)).astype(u.dtype)
    return out, vn.astype(u.dtype)


def _layer(x, prev_row, S0, chunk, lp):
    h = _rms_norm(x, lp['g_pre_mix'])
    p = h @ lp['w_in']
    o = SHIFT_W
    p_shift = p[..., :o]
    pu = p[..., o:o + GMLP_W]; o += GMLP_W
    pv = p[..., o:o + GMLP_W]; o += GMLP_W
    pga = p[..., o:o + D_MODEL]; o += D_MODEL
    pgb = p[..., o:o + D_MODEL]
    o_a, S, last_row = _rwkv7(p_shift, prev_row, S0, lp)
    o_b, vn = _gmlp(pu, pv, chunk, lp)
    m = jax.nn.sigmoid(pga) * (o_a @ lp['w_proj_a']) + jax.nn.sigmoid(pgb) * (o_b @ lp['w_proj_b'])
    x = x + _rms_norm(m @ lp['w_o'], lp['g_post_mix'])
    f = _rms_norm(x, lp['g_pre_ffn'])
    f = (jax.nn.silu(f @ lp['w_gate']) * (f @ lp['w_up'])) @ lp['w_down']
    x = x + _rms_norm(f, lp['g_post_ffn'])
    return x, last_row, S.astype(x.dtype), vn


def setup_inputs(seed: int = 0) -> dict:
    key = jax.random.key(seed)
    ks = jax.random.split(key, 32)
    nrm = lambda k, shape, s: jax.random.normal(k, shape, jnp.float32) * s
    L = DEPTH
    return {
        "x_prompt": nrm(ks[0], (BATCH, SEQ, D_MODEL), 1.0),
        "x_sample": nrm(ks[1], (DEC_BATCH, DEC_SEQ, D_MODEL), 1.0),
        "state_shift": nrm(ks[2], (L, DEC_BATCH, 1, SHIFT_W), 1.0),
        "state_wkv": nrm(ks[3], (L, DEC_BATCH, N_HEADS, HEAD_DIM, HEAD_DIM), 0.1),
        "g_pre_mix": 1.0 + nrm(ks[4], (L, D_MODEL), 0.05),
        "w_in": nrm(ks[5], (L, D_MODEL, IN_W), D_MODEL ** -0.5),
        "mu_shift": jax.random.uniform(ks[6], (L, SHIFT_W), jnp.float32),
        "w0": jax.random.uniform(ks[7], (L, RWKV_W), jnp.float32, -4.0, -0.5),
        "w_lora_w": nrm(ks[8], (L, LORA_W, RWKV_W), 0.5 * LORA_W ** -0.5),
        "a0": nrm(ks[9], (L, RWKV_W), 0.1),
        "w_lora_a": nrm(ks[10], (L, LORA_A, RWKV_W), 0.5 * LORA_A ** -0.5),
        "w_lora_g": nrm(ks[11], (L, LORA_G, RWKV_W), LORA_G ** -0.5),
        "k_k": 0.85 + nrm(ks[12], (L, RWKV_W), 0.05),
        "k_a": 1.0 + nrm(ks[13], (L, RWKV_W), 0.05),
        "r_k": nrm(ks[14], (L, N_HEADS, HEAD_DIM), 0.1),
        "gn_g": 1.0 + nrm(ks[15], (L, RWKV_W), 0.05),
        "gn_b": nrm(ks[16], (L, RWKV_W), 0.01),
        "ln_v_g": 1.0 + nrm(ks[17], (L, GMLP_W), 0.05),
        "ln_v_b": nrm(ks[18], (L, GMLP_W), 0.01),
        "w_spatial": nrm(ks[19], (L, GMLP_GROUPS, GMLP_CHUNK, GMLP_CHUNK), GMLP_CHUNK ** -0.5),
        "b_spatial": 1.0 + nrm(ks[20], (L, GMLP_GROUPS, GMLP_CHUNK), 0.05),
        "w_proj_a": nrm(ks[21], (L, RWKV_W, D_MODEL), RWKV_W ** -0.5),
        "w_proj_b": nrm(ks[22], (L, GMLP_W, D_MODEL), GMLP_W ** -0.5),
        "w_o": nrm(ks[23], (L, D_MODEL, D_MODEL), D_MODEL ** -0.5),
        "g_post_mix": 1.0 + nrm(ks[24], (L, D_MODEL), 0.05),
        "g_pre_ffn": 1.0 + nrm(ks[25], (L, D_MODEL), 0.05),
        "w_gate": nrm(ks[26], (L, D_MODEL, D_FF), D_MODEL ** -0.5),
        "w_up": nrm(ks[27], (L, D_MODEL, D_FF), D_MODEL ** -0.5),
        "w_down": nrm(ks[28], (L, D_FF, D_MODEL), D_FF ** -0.5),
        "g_post_ffn": 1.0 + nrm(ks[29], (L, D_MODEL), 0.05),
    }


def reference(x_prompt, x_sample, state_shift, state_wkv, g_pre_mix, w_in, mu_shift, w0, w_lora_w, a0,
              w_lora_a, w_lora_g, k_k, k_a, r_k, gn_g, gn_b, ln_v_g, ln_v_b, w_spatial, b_spatial,
              w_proj_a, w_proj_b, w_o, g_post_mix, g_pre_ffn, w_gate, w_up, w_down, g_post_ffn):
    yp, ys = x_prompt, x_sample
    shift_p, wkv_p, shift_s, wkv_s, v_s = [], [], [], [], []
    Bp = x_prompt.shape[0]
    for l in range(DEPTH):
        lp = {
            'g_pre_mix': g_pre_mix[l], 'w_in': w_in[l], 'mu_shift': mu_shift[l], 'w0': w0[l],
            'w_lora_w': w_lora_w[l], 'a0': a0[l], 'w_lora_a': w_lora_a[l], 'w_lora_g': w_lora_g[l],
            'k_k': k_k[l], 'k_a': k_a[l], 'r_k': r_k[l], 'gn_g': gn_g[l], 'gn_b': gn_b[l],
            'ln_v_g': ln_v_g[l], 'ln_v_b': ln_v_b[l], 'w_spatial': w_spatial[l], 'b_spatial': b_spatial[l],
            'w_proj_a': w_proj_a[l], 'w_proj_b': w_proj_b[l], 'w_o': w_o[l], 'g_post_mix': g_post_mix[l],
            'g_pre_ffn': g_pre_ffn[l], 'w_gate': w_gate[l], 'w_up': w_up[l], 'w_down': w_down[l],
            'g_post_ffn': g_post_ffn[l],
        }
        zero_row = jnp.zeros((Bp, 1, SHIFT_W), x_prompt.dtype)
        zero_S = jnp.zeros((Bp, N_HEADS, HEAD_DIM, HEAD_DIM), jnp.float32)
        yp, sp, wp, _ = _layer(yp, zero_row, zero_S, GMLP_CHUNK, lp)
        ys, ss, wsm, vs = _layer(ys, state_shift[l], state_wkv[l], x_sample.shape[1], lp)
        shift_p.append(sp); wkv_p.append(wp); shift_s.append(ss); wkv_s.append(wsm); v_s.append(vs)
    return (yp, ys, jnp.stack(shift_p), jnp.stack(wkv_p), jnp.stack(shift_s), jnp.stack(wkv_s), jnp.stack(v_s))
```

```python
import functools

import jax
import jax.numpy as jnp
from jax import lax
from jax.experimental import pallas as pl
from jax.experimental.pallas import tpu as pltpu

F32 = jnp.float32
BF16 = jnp.bfloat16

D_MODEL = 1024
HEAD_DIM = 64
N_HEADS = D_MODEL // HEAD_DIM
RWKV_W = N_HEADS * HEAD_DIM
LORA_W = 64
LORA_A = 64
LORA_G = 128
SHIFT_W = 3 * RWKV_W + LORA_W + LORA_A + LORA_G
GMLP_W = D_MODEL
GMLP_GC = 128
GMLP_CHUNK = 128
GMLP_GROUPS = GMLP_W // GMLP_GC
IN_W = SHIFT_W + 2 * GMLP_W + 2 * D_MODEL
RMS_EPS = 1e-6
GN_EPS = 64e-5
LN_EPS = 1e-5

MXU_TILE = 256
WKV_CHUNK = 64
HEADS_PER_GROUP = MXU_TILE // HEAD_DIM
N_GROUPS = RWKV_W // MXU_TILE
MIX_ROWS = 256
FFN_ROWS = 512
VMEM_LIMIT = 60 * 1024 * 1024

_NEUMANN_STEPS = 4


def _dot(a, b):
    return jnp.dot(a.astype(BF16), b.astype(BF16), preferred_element_type=F32)


def _dot_nt(a, b):
    return lax.dot_general(a.astype(BF16), b.astype(BF16), (((1,), (1,)), ((), ())),
                           preferred_element_type=F32)


def _block_id(idx, block):
    assert block & (block - 1) == 0
    return lax.shift_right_logical(idx, block.bit_length() - 1)


def _rms(x, g):
    return x * lax.rsqrt(jnp.mean(x * x, axis=-1, keepdims=True) + RMS_EPS) * g


def _split2(x):
    hi = x.astype(BF16)
    lo = (x - hi.astype(F32)).astype(BF16)
    return hi, lo


def _head_sum(x, ones_bd):
    hi, lo = _split2(x)
    cols = []
    for g in range(N_GROUPS):
        sl = slice(g * MXU_TILE, (g + 1) * MXU_TILE)
        cols.append(jnp.dot(hi[:, sl], ones_bd, preferred_element_type=F32)
                    + jnp.dot(lo[:, sl], ones_bd, preferred_element_type=F32))
    return jnp.concatenate(cols, axis=1)


def _mix_kernel(x_ref, shift0_ref, wkv0_ref, gpre_ref, win_ref, mu_ref, w0_ref, wwa_ref, a0_ref, wg_ref,
                kk_ref, ka_ref, rk_ref, gng_ref, gnb_ref, lng_ref, lnb_ref, ws_ref, bias_ref,
                wpa_ref, wpb_ref, wo_ref, gpost_ref,
                xo_ref, shifto_ref, wkvo_ref, *rest, n_seq, seq_rows, gmlp_chunk, emit_vn):
    if emit_vn:
        vno_ref, rest = rest[0], rest[1:]
    (carry_ref, state_ref, at_ref, bt_ref, kt_ref, rt_ref, vb_ref, bh_ref, kh_ref, pc_ref, y_ref) = rest
    rows = n_seq * seq_rows
    n_chunks = rows // WKV_CHUNK
    chunks_per_seq = seq_rows // WKV_CHUNK
    t = pl.program_id(1)

    same_head = (_block_id(lax.broadcasted_iota(jnp.int32, (MXU_TILE, MXU_TILE), 0), HEAD_DIM)
                 == _block_id(lax.broadcasted_iota(jnp.int32, (MXU_TILE, MXU_TILE), 1), HEAD_DIM))
    bd_f32 = jnp.where(same_head, 1.0, 0.0)
    ones_bf = bd_f32.astype(BF16)

    @pl.when(t == 0)
    def _():
        carry_ref[...] = shift0_ref[...]
        for s in range(n_seq):
            for g in range(N_GROUPS):
                c = wkv0_ref[s, g]
                state_ref[s, g] = jnp.concatenate([c] * HEADS_PER_GROUP, axis=1) * bd_f32

    x = x_ref[...].reshape(rows, D_MODEL)
    h = _rms(x, gpre_ref[...]).astype(BF16)

    ps = jnp.dot(h, win_ref[:, 0:SHIFT_W], preferred_element_type=F32)
    row_id = lax.broadcasted_iota(jnp.int32, (rows, 1), 0)
    p_prev = pltpu.roll(ps, 1, 0)
    for s in range(n_seq):
        p_prev = jnp.where(row_id == s * seq_rows, carry_ref[s], p_prev)
        last = ps[(s + 1) * seq_rows - 1:(s + 1) * seq_rows, :]
        carry_ref[s] = last
        shifto_ref[s] = last
    xm = ps + (p_prev - ps) * mu_ref[...]
    r = xm[:, 0:RWKV_W]
    k = xm[:, RWKV_W:2 * RWKV_W]
    v = xm[:, 2 * RWKV_W:3 * RWKV_W]
    o = 3 * RWKV_W
    dwa = xm[:, o:o + LORA_W + LORA_A]
    dg = xm[:, o + LORA_W + LORA_A:SHIFT_W]
    lane = lax.broadcasted_iota(jnp.int32, (1, LORA_W + LORA_A), 1)
    dwa = jnp.where(lane < LORA_W, jnp.tanh(dwa), dwa)
    lwa = _dot(dwa, wwa_ref[...])
    zw = w0_ref[...] + lwa[:, 0:RWKV_W]
    w_log = -(jnp.maximum(-zw, 0.0) + jnp.log1p(jnp.exp(-jnp.abs(zw)))) - 0.5
    lw = -jnp.exp(w_log)
    a_rate = jax.nn.sigmoid(a0_ref[...] + lwa[:, RWKV_W:2 * RWKV_W])
    g_gate = _dot(jax.nn.sigmoid(dg), wg_ref[...])

    kk = k * kk_ref[...]
    kk = kk * lax.rsqrt(jnp.maximum(_head_sum(kk * kk, ones_bf), 1e-24))
    k_mod = k * (1.0 + (a_rate - 1.0) * ka_ref[...])
    bonus = _head_sum(r * k_mod * rk_ref[...], ones_bf) * v

    ri = lax.broadcasted_iota(jnp.int32, (rows, rows), 0)
    ci = lax.broadcasted_iota(jnp.int32, (rows, rows), 1)
    same_chunk = _block_id(ri, WKV_CHUNK) == _block_id(ci, WKV_CHUNK)
    tri = jnp.where(same_chunk & (ci <= ri), 1.0, 0.0).astype(BF16)
    tri_up = jnp.where(same_chunk & (ci > ri), 1.0, 0.0).astype(BF16)
    lw_hi, lw_lo = _split2(lw)
    cs = (jnp.dot(tri, lw_hi, preferred_element_type=F32)
          + jnp.dot(tri, lw_lo, preferred_element_type=F32))
    rev = (jnp.dot(tri_up, lw_hi, preferred_element_type=F32)
           + jnp.dot(tri_up, lw_lo, preferred_element_type=F32))
    to_end = jnp.exp(rev)
    pc_ref[...] = jnp.exp(cs + rev)
    p_inv = jnp.exp(-cs)
    at_ref[...] = (-kk * jnp.exp(cs - lw)).astype(BF16)
    b_vec = kk * a_rate
    bt_ref[...] = (b_vec * p_inv).astype(BF16)
    kt_ref[...] = (k_mod * p_inv).astype(BF16)
    rt_ref[...] = (r * jnp.exp(cs)).astype(BF16)
    vb_ref[...] = v.astype(BF16)
    bh_ref[...] = (b_vec * to_end).astype(BF16)
    kh_ref[...] = (k_mod * to_end).astype(BF16)

    bd_bf = ones_bf
    tq = lax.broadcasted_iota(jnp.int32, (WKV_CHUNK, MXU_TILE), 0)
    sq = lax.broadcasted_iota(jnp.int32, (WKV_CHUNK, MXU_TILE), 1) & (WKV_CHUNK - 1)
    strict = sq < tq
    incl = sq <= tq
    eye_all = jnp.where(sq == tq, 1.0, 0.0)

    def bdiag(m):
        m = m.astype(BF16)
        return jnp.concatenate([m] * HEADS_PER_GROUP, axis=0) * bd_bf

    def chunk_body(i, carry):
        r0 = pl.multiple_of(i * WKV_CHUNK, WKV_CHUNK)
        rsl = pl.ds(r0, WKV_CHUNK)
        sidx = i // chunks_per_seq
        for g in range(N_GROUPS):
            ls = slice(g * MXU_TILE, (g + 1) * MXU_TILE)
            at = at_ref[rsl, ls]
            rt = rt_ref[rsl, ls]
            vv = vb_ref[rsl, ls]
            lhs = jnp.concatenate([at, rt], axis=0)
            s1 = _dot_nt(lhs, bdiag(bt_ref[rsl, ls]))
            s2 = _dot_nt(lhs, bdiag(kt_ref[rsl, ls]))
            l_ab = jnp.where(strict, s1[:WKV_CHUNK], 0.0)
            a_rb = jnp.where(incl, s1[WKV_CHUNK:], 0.0).astype(BF16)
            l_ak = jnp.where(strict, s2[:WKV_CHUNK], 0.0)
            a_rk = jnp.where(incl, s2[WKV_CHUNK:], 0.0)
            l_pow = _dot(l_ab, bdiag(l_ab))
            t_inv = eye_all + l_ab
            for _ in range(_NEUMANN_STEPS):
                out = _dot(jnp.concatenate([t_inv.astype(BF16), l_pow.astype(BF16)], axis=0), bdiag(l_pow))
                t_inv = t_inv + out[:WKV_CHUNK]
                l_pow = out[WKV_CHUNK:]
            t_inv = (t_inv + _dot(t_inv, bdiag(l_pow))).astype(BF16)
            out = _dot(jnp.concatenate([l_ak.astype(BF16), a_rk.astype(BF16)], axis=0), bdiag(vv))
            x_mid = out[:WKV_CHUNK]
            av = out[WKV_CHUNK:]
            w_mat = _dot(t_inv, bdiag(at))
            u0 = _dot(t_inv, bdiag(x_mid))
            q_hat = rt.astype(F32) + _dot(a_rb, bdiag(w_mat))
            y0 = _dot(a_rb, bdiag(u0)) + av
            st = state_ref[sidx, g]
            out = _dot_nt(jnp.concatenate([q_hat.astype(BF16), w_mat.astype(BF16)], axis=0), st)
            y_ref[rsl, ls] = out[:WKV_CHUNK] + y0
            u_mat = out[WKV_CHUNK:] + u0
            uv = jnp.concatenate([u_mat, vv.astype(F32)], axis=0)
            bk = jnp.concatenate([bh_ref[rsl, ls], kh_ref[rsl, ls]], axis=0)
            upd = _dot(uv.T, bk)
            pc = pc_ref[pl.ds(r0, 8), ls][0:1]
            state_ref[sidx, g] = st * pc + upd * bd_f32
        return carry

    lax.fori_loop(0, n_chunks, chunk_body, 0)

    @pl.when(t == pl.num_programs(1) - 1)
    def _():
        for s in range(n_seq):
            for g in range(N_GROUPS):
                st = state_ref[s, g]
                acc = st[:, 0:HEAD_DIM]
                for j in range(1, HEADS_PER_GROUP):
                    acc = acc + st[:, j * HEAD_DIM:(j + 1) * HEAD_DIM]
                wkvo_ref[s, g] = acc

    y = y_ref[...]
    inv_hd = 1.0 / HEAD_DIM
    mean = _head_sum(y, ones_bf) * inv_hd
    yc = y - mean
    var = _head_sum(yc * yc, ones_bf) * inv_hd
    yn = yc * lax.rsqrt(var + GN_EPS) * gng_ref[...] + gnb_ref[...]
    o_a = (yn + bonus) * g_gate
    proj_a = _dot(o_a, wpa_ref[...])

    puv = jnp.dot(h, win_ref[:, SHIFT_W:SHIFT_W + 2 * GMLP_W], preferred_element_type=F32)
    pu = puv[:, 0:GMLP_W]
    pv = puv[:, GMLP_W:2 * GMLP_W]
    mu_v = jnp.mean(pv, axis=-1, keepdims=True)
    pvc = pv - mu_v
    var_v = jnp.mean(pvc * pvc, axis=-1, keepdims=True)
    vn = pvc * lax.rsqrt(var_v + LN_EPS) * lng_ref[...] + lnb_ref[...]
    if emit_vn:
        vno_ref[...] = vn.reshape(vno_ref.shape)
    vn_bf = vn.astype(BF16)
    sp_mask = (_block_id(ri, gmlp_chunk) == _block_id(ci, gmlp_chunk)) & (ci <= ri)
    zs = []
    for g in range(GMLP_GROUPS):
        ws_g = jnp.where(sp_mask, ws_ref[g], 0.0).astype(BF16)
        zs.append(jnp.dot(ws_g, vn_bf[:, g * GMLP_GC:(g + 1) * GMLP_GC], preferred_element_type=F32))
    z = jnp.concatenate(zs, axis=1) + bias_ref[...]
    o_b = pu * z
    proj_b = _dot(o_b, wpb_ref[...])

    pg = jnp.dot(h, win_ref[:, SHIFT_W + 2 * GMLP_W:IN_W], preferred_element_type=F32)
    m = jax.nn.sigmoid(pg[:, 0:D_MODEL]) * proj_a + jax.nn.sigmoid(pg[:, D_MODEL:2 * D_MODEL]) * proj_b
    mo = _dot(m, wo_ref[...])
    xo_ref[...] = (x + _rms(mo, gpost_ref[...])).reshape(xo_ref.shape)


def _ffn_kernel(x_ref, gpre_ref, wgate_ref, wup_ref, wdown_ref, gpost_ref, o_ref):
    x = x_ref[...]
    f = _rms(x, gpre_ref[...]).astype(BF16)
    gate = jnp.dot(f, wgate_ref[...], preferred_element_type=F32)
    up = jnp.dot(f, wup_ref[...], preferred_element_type=F32)
    hid = (gate * jax.nn.sigmoid(gate) * up).astype(BF16)
    f2 = jnp.dot(hid, wdown_ref[...], preferred_element_type=F32)
    o_ref[...] = x + _rms(f2, gpost_ref[...])


def _resident(shape):
    nd = len(shape)
    return pl.BlockSpec(shape, lambda *_: (0,) * nd, pipeline_mode=pl.Buffered(1))


def _mix_call(x, shift0, wkv0_bd, params, *, n_seq, seq_rows, gmlp_chunk, emit_vn):
    batch, seq_len, _ = x.shape
    rows = n_seq * seq_rows
    assert rows == MIX_ROWS and batch % n_seq == 0 and seq_len % seq_rows == 0
    assert seq_rows % gmlp_chunk == 0 and seq_rows % WKV_CHUNK == 0
    grid = (batch // n_seq, seq_len // seq_rows)

    ws = jnp.tile(params['w_spatial'][:, :gmlp_chunk, :gmlp_chunk], (1, rows // gmlp_chunk, rows // gmlp_chunk))
    bias = jnp.tile(jnp.repeat(params['b_spatial'][:, :gmlp_chunk].T, GMLP_GC, axis=1), (rows // gmlp_chunk, 1))
    weights = [params['g_pre_mix'], params['w_in'], params['mu_shift'], params['w0'], params['w_wa'], params['a0'],
               params['w_lora_g'], params['k_k'], params['k_a'], params['r_k'], params['gn_g'], params['gn_b'],
               params['ln_v_g'], params['ln_v_b'], ws, bias, params['w_proj_a'], params['w_proj_b'], params['w_o'],
               params['g_post_mix']]

    x_spec = pl.BlockSpec((n_seq, seq_rows, D_MODEL), lambda b, t: (b, t, 0))
    per_seq = lambda shape: pl.BlockSpec((n_seq,) + shape, lambda b, t: (b,) + (0,) * len(shape))
    in_specs = [x_spec, per_seq((1, SHIFT_W)), per_seq((N_GROUPS, MXU_TILE, HEAD_DIM))]
    in_specs += [_resident(w.shape) for w in weights]
    out_shape = [jax.ShapeDtypeStruct(x.shape, F32),
                 jax.ShapeDtypeStruct((batch, 1, SHIFT_W), F32),
                 jax.ShapeDtypeStruct((batch, N_GROUPS, MXU_TILE, HEAD_DIM), F32)]
    out_specs = [x_spec, per_seq((1, SHIFT_W)), per_seq((N_GROUPS, MXU_TILE, HEAD_DIM))]
    if emit_vn:
        out_shape.append(jax.ShapeDtypeStruct((batch, seq_len, GMLP_W), F32))
        out_specs.append(x_spec)
    act = lambda dt: pltpu.VMEM((rows, RWKV_W), dt)
    scratch = [pltpu.VMEM((n_seq, 1, SHIFT_W), F32),
               pltpu.VMEM((n_seq, N_GROUPS, MXU_TILE, MXU_TILE), F32),
               act(BF16), act(BF16), act(BF16), act(BF16), act(BF16), act(BF16), act(BF16),
               act(F32), act(F32)]
    kern = functools.partial(_mix_kernel, n_seq=n_seq, seq_rows=seq_rows, gmlp_chunk=gmlp_chunk, emit_vn=emit_vn)
    return pl.pallas_call(
        kern, grid=grid, in_specs=in_specs, out_specs=out_specs, out_shape=out_shape, scratch_shapes=scratch,
        compiler_params=pltpu.CompilerParams(dimension_semantics=("arbitrary", "arbitrary"),
                                             vmem_limit_bytes=VMEM_LIMIT),
    )(x, shift0, wkv0_bd, *weights)


def _ffn_call(x2d, params):
    n = x2d.shape[0]
    assert n % FFN_ROWS == 0
    weights = [params['g_pre_ffn'], params['w_gate'], params['w_up'], params['w_down'], params['g_post_ffn']]
    row_spec = pl.BlockSpec((FFN_ROWS, D_MODEL), lambda i: (i, 0))
    return pl.pallas_call(
        _ffn_kernel, grid=(n // FFN_ROWS,), in_specs=[row_spec] + [_resident(w.shape) for w in weights],
        out_specs=row_spec, out_shape=jax.ShapeDtypeStruct(x2d.shape, F32),
        compiler_params=pltpu.CompilerParams(dimension_semantics=("arbitrary",), vmem_limit_bytes=VMEM_LIMIT),
    )(x2d, *weights)


def _layer(x, shift0, wkv0, params, *, n_seq, seq_rows, gmlp_chunk, emit_vn):
    wkv0_g = wkv0.astype(F32).reshape(wkv0.shape[0], N_GROUPS, MXU_TILE, HEAD_DIM)
    outs = _mix_call(x, shift0, wkv0_g, params, n_seq=n_seq, seq_rows=seq_rows,
                     gmlp_chunk=gmlp_chunk, emit_vn=emit_vn)
    x_mid, shift, wkv_g = outs[:3]
    y = _ffn_call(x_mid.reshape(-1, D_MODEL), params).reshape(x.shape)
    return y, shift, wkv_g.reshape(wkv0.shape), (outs[3] if emit_vn else None)


def kernel(x_prompt, x_sample, state_shift, state_wkv, g_pre_mix, w_in, mu_shift, w0, w_lora_w, a0, w_lora_a, w_lora_g, k_k, k_a, r_k, gn_g, gn_b, ln_v_g, ln_v_b, w_spatial, b_spatial, w_proj_a, w_proj_b, w_o, g_post_mix, g_pre_ffn, w_gate, w_up, w_down, g_post_ffn):
    depth = w_in.shape[0]
    yp, ys = x_prompt, x_sample
    bp = x_prompt.shape[0]
    shift_p, wkv_p, shift_s, wkv_s, v_s = [], [], [], [], []
    for l in range(depth):
        row = lambda a: a[l].reshape(1, -1).astype(F32)
        zero = jnp.zeros((LORA_W, RWKV_W), BF16)
        w_wa = jnp.concatenate(
            [jnp.concatenate([w_lora_w[l].astype(BF16), zero], axis=1),
             jnp.concatenate([zero, w_lora_a[l].astype(BF16)], axis=1)], axis=0)
        params = {
            'g_pre_mix': row(g_pre_mix), 'w_in': w_in[l].astype(BF16), 'mu_shift': row(mu_shift), 'w0': row(w0),
            'w_wa': w_wa, 'a0': row(a0), 'w_lora_g': w_lora_g[l].astype(BF16), 'k_k': row(k_k), 'k_a': row(k_a),
            'r_k': row(r_k), 'gn_g': row(gn_g), 'gn_b': row(gn_b), 'ln_v_g': row(ln_v_g), 'ln_v_b': row(ln_v_b),
            'w_spatial': w_spatial[l].astype(F32), 'b_spatial': b_spatial[l].astype(F32),
            'w_proj_a': w_proj_a[l].astype(BF16), 'w_proj_b': w_proj_b[l].astype(BF16), 'w_o': w_o[l].astype(BF16),
            'g_post_mix': row(g_post_mix), 'g_pre_ffn': row(g_pre_ffn), 'w_gate': w_gate[l].astype(BF16),
            'w_up': w_up[l].astype(BF16), 'w_down': w_down[l].astype(BF16), 'g_post_ffn': row(g_post_ffn),
        }
        zero_row = jnp.zeros((bp, 1, SHIFT_W), F32)
        zero_s = jnp.zeros((bp, N_HEADS, HEAD_DIM, HEAD_DIM), F32)
        yp, sp, wp, _ = _layer(yp, zero_row, zero_s, params, n_seq=1, seq_rows=MIX_ROWS,
                               gmlp_chunk=GMLP_CHUNK, emit_vn=False)
        dec_seq = x_sample.shape[1]
        ys, ss, wsm, vs = _layer(ys, state_shift[l], state_wkv[l], params, n_seq=MIX_ROWS // dec_seq,
                                 seq_rows=dec_seq, gmlp_chunk=dec_seq, emit_vn=True)
        shift_p.append(sp); wkv_p.append(wp); shift_s.append(ss); wkv_s.append(wsm); v_s.append(vs)
    return (yp, ys, jnp.stack(shift_p), jnp.stack(wkv_p), jnp.stack(shift_s), jnp.stack(wkv_s), jnp.stack(v_s))
```

```python
import functools

import jax
import jax.numpy as jnp
from jax import lax
from jax.experimental import pallas as pl
from jax.experimental.pallas import tpu as pltpu

F32 = jnp.float32
BF16 = jnp.bfloat16

D_MODEL = 1024
HEAD_DIM = 64
N_HEADS = D_MODEL // HEAD_DIM
RWKV_W = N_HEADS * HEAD_DIM
LORA_W = 64
LORA_A = 64
LORA_G = 128
SHIFT_W = 3 * RWKV_W + LORA_W + LORA_A + LORA_G
GMLP_W = D_MODEL
GMLP_GC = 128
GMLP_CHUNK = 128
GMLP_GROUPS = GMLP_W // GMLP_GC
IN_W = SHIFT_W + 2 * GMLP_W + 2 * D_MODEL
RMS_EPS = 1e-6
GN_EPS = 64e-5
LN_EPS = 1e-5

MXU_TILE = 256
WKV_CHUNK = 64
HEADS_PER_GROUP = MXU_TILE // HEAD_DIM
N_GROUPS = RWKV_W // MXU_TILE
MIX_ROWS = 256
FFN_ROWS = 512
VMEM_LIMIT = 60 * 1024 * 1024

_NEUMANN_STEPS = 4


def _dot(a, b):
    return jnp.dot(a.astype(BF16), b.astype(BF16), preferred_element_type=F32)


def _dot_nt(a, b):
    return lax.dot_general(a.astype(BF16), b.astype(BF16), (((1,), (1,)), ((), ())),
                           preferred_element_type=F32)


def _block_id(idx, block):
    assert block & (block - 1) == 0
    return lax.shift_right_logical(idx, block.bit_length() - 1)


def _rms(x, g):
    return x * lax.rsqrt(jnp.mean(x * x, axis=-1, keepdims=True) + RMS_EPS) * g


def _split2(x):
    hi = x.astype(BF16)
    lo = (x - hi.astype(F32)).astype(BF16)
    return hi, lo


def _head_sum(x, ones_bd):
    hi, lo = _split2(x)
    cols = []
    for g in range(N_GROUPS):
        sl = slice(g * MXU_TILE, (g + 1) * MXU_TILE)
        cols.append(jnp.dot(hi[:, sl], ones_bd, preferred_element_type=F32)
                    + jnp.dot(lo[:, sl], ones_bd, preferred_element_type=F32))
    return jnp.concatenate(cols, axis=1)


def _mix_kernel(x_ref, shift0_ref, wkv0_ref, gpre_ref, win_ref, mu_ref, w0_ref, wwa_ref, a0_ref, wg_ref,
                kk_ref, ka_ref, rk_ref, gng_ref, gnb_ref, lng_ref, lnb_ref, ws_ref, bias_ref,
                wpa_ref, wpb_ref, wo_ref, gpost_ref,
                xo_ref, shifto_ref, wkvo_ref, *rest, n_seq, seq_rows, gmlp_chunk, emit_vn):
    if emit_vn:
        vno_ref, rest = rest[0], rest[1:]
    (carry_ref, state_ref, at_ref, bt_ref, kt_ref, rt_ref, vb_ref, bh_ref, kh_ref, pc_ref, y_ref) = rest
    rows = n_seq * seq_rows
    n_chunks = rows // WKV_CHUNK
    chunks_per_seq = seq_rows // WKV_CHUNK
    t = pl.program_id(1)

    same_head = (_block_id(lax.broadcasted_iota(jnp.int32, (MXU_TILE, MXU_TILE), 0), HEAD_DIM)
                 == _block_id(lax.broadcasted_iota(jnp.int32, (MXU_TILE, MXU_TILE), 1), HEAD_DIM))
    bd_f32 = jnp.where(same_head, 1.0, 0.0)
    ones_bf = bd_f32.astype(BF16)

    @pl.when(t == 0)
    def _():
        carry_ref[...] = shift0_ref[...]
        for s in range(n_seq):
            for g in range(N_GROUPS):
                c = wkv0_ref[s, g]
                state_ref[s, g] = jnp.concatenate([c] * HEADS_PER_GROUP, axis=1) * bd_f32

    x = x_ref[...].reshape(rows, D_MODEL)
    h = _rms(x, gpre_ref[...]).astype(BF16)

    ps = jnp.dot(h, win_ref[:, 0:SHIFT_W], preferred_element_type=F32)
    row_id = lax.broadcasted_iota(jnp.int32, (rows, 1), 0)
    p_prev = pltpu.roll(ps, 1, 0)
    for s in range(n_seq):
        p_prev = jnp.where(row_id == s * seq_rows, carry_ref[s], p_prev)
        last = ps[(s + 1) * seq_rows - 1:(s + 1) * seq_rows, :]
        carry_ref[s] = last
        shifto_ref[s] = last
    xm = ps + (p_prev - ps) * mu_ref[...]
    r = xm[:, 0:RWKV_W]
    k = xm[:, RWKV_W:2 * RWKV_W]
    v = xm[:, 2 * RWKV_W:3 * RWKV_W]
    o = 3 * RWKV_W
    dwa = xm[:, o:o + LORA_W + LORA_A]
    dg = xm[:, o + LORA_W + LORA_A:SHIFT_W]
    lane = lax.broadcasted_iota(jnp.int32, (1, LORA_W + LORA_A), 1)
    dwa = jnp.where(lane < LORA_W, jnp.tanh(dwa), dwa)
    lwa = _dot(dwa, wwa_ref[...])
    zw = w0_ref[...] + lwa[:, 0:RWKV_W]
    w_log = -(jnp.maximum(-zw, 0.0) + jnp.log1p(jnp.exp(-jnp.abs(zw)))) - 0.5
    lw = -jnp.exp(w_log)
    a_rate = jax.nn.sigmoid(a0_ref[...] + lwa[:, RWKV_W:2 * RWKV_W])
    g_gate = _dot(jax.nn.sigmoid(dg), wg_ref[...])

    kk = k * kk_ref[...]
    kk = kk * lax.rsqrt(jnp.maximum(_head_sum(kk * kk, ones_bf), 1e-24))
    k_mod = k * (1.0 + (a_rate - 1.0) * ka_ref[...])
    bonus = _head_sum(r * k_mod * rk_ref[...], ones_bf) * v

    ri = lax.broadcasted_iota(jnp.int32, (rows, rows), 0)
    ci = lax.broadcasted_iota(jnp.int32, (rows, rows), 1)
    same_chunk = _block_id(ri, WKV_CHUNK) == _block_id(ci, WKV_CHUNK)
    tri = jnp.where(same_chunk & (ci <= ri), 1.0, 0.0).astype(BF16)
    tri_up = jnp.where(same_chunk & (ci > ri), 1.0, 0.0).astype(BF16)
    lw_hi, lw_lo = _split2(lw)
    cs = (jnp.dot(tri, lw_hi, preferred_element_type=F32)
          + jnp.dot(tri, lw_lo, preferred_element_type=F32))
    rev = (jnp.dot(tri_up, lw_hi, preferred_element_type=F32)
           + jnp.dot(tri_up, lw_lo, preferred_element_type=F32))
    to_end = jnp.exp(rev)
    cs_end = (cs + rev).reshape(n_chunks, WKV_CHUNK, RWKV_W)[:, 0:8, :]
    pc_ref[...] = jnp.exp(cs_end).reshape(n_chunks * 8, RWKV_W)
    p_inv = jnp.exp(-cs)
    at_ref[...] = (-kk * jnp.exp(cs - lw)).astype(BF16)
    b_vec = kk * a_rate
    bt_ref[...] = (b_vec * p_inv).astype(BF16)
    kt_ref[...] = (k_mod * p_inv).astype(BF16)
    rt_ref[...] = (r * jnp.exp(cs)).astype(BF16)
    vb_ref[...] = v.astype(BF16)
    bh_ref[...] = (b_vec * to_end).astype(BF16)
    kh_ref[...] = (k_mod * to_end).astype(BF16)

    bd_bf = ones_bf
    tq = lax.broadcasted_iota(jnp.int32, (WKV_CHUNK, MXU_TILE), 0)
    sq = lax.broadcasted_iota(jnp.int32, (WKV_CHUNK, MXU_TILE), 1) & (WKV_CHUNK - 1)
    strict = sq < tq
    incl = sq <= tq
    eye_all = jnp.where(sq == tq, 1.0, 0.0)

    def bdiag(m):
        m = m.astype(BF16)
        return jnp.concatenate([m] * HEADS_PER_GROUP, axis=0) * bd_bf

    def chunk_body(i, carry):
        r0 = pl.multiple_of(i * WKV_CHUNK, WKV_CHUNK)
        rsl = pl.ds(r0, WKV_CHUNK)
        sidx = i // chunks_per_seq
        groups = range(N_GROUPS)
        lanes = [slice(g * MXU_TILE, (g + 1) * MXU_TILE) for g in groups]
        stack = lambda top, bottom: jnp.concatenate([top.astype(BF16), bottom.astype(BF16)], axis=0)
        at = [at_ref[rsl, ls] for ls in lanes]
        rt = [rt_ref[rsl, ls] for ls in lanes]
        vv = [vb_ref[rsl, ls] for ls in lanes]
        lhs = [stack(at[g], rt[g]) for g in groups]
        s1 = [_dot_nt(lhs[g], bdiag(bt_ref[rsl, lanes[g]])) for g in groups]
        s2 = [_dot_nt(lhs[g], bdiag(kt_ref[rsl, lanes[g]])) for g in groups]
        l_ab = [jnp.where(strict, s[:WKV_CHUNK], 0.0) for s in s1]
        a_rb = [jnp.where(incl, s[WKV_CHUNK:], 0.0).astype(BF16) for s in s1]
        l_ak = [jnp.where(strict, s[:WKV_CHUNK], 0.0) for s in s2]
        a_rk = [jnp.where(incl, s[WKV_CHUNK:], 0.0) for s in s2]
        l_pow = [_dot(l_ab[g], bdiag(l_ab[g])) for g in groups]
        t_inv = [eye_all + l_ab[g] for g in groups]
        for _ in range(_NEUMANN_STEPS):
            out = [_dot(stack(t_inv[g], l_pow[g]), bdiag(l_pow[g])) for g in groups]
            t_inv = [t_inv[g] + out[g][:WKV_CHUNK] for g in groups]
            l_pow = [out[g][WKV_CHUNK:] for g in groups]
        t_inv = [(t_inv[g] + _dot(t_inv[g], bdiag(l_pow[g]))).astype(BF16) for g in groups]
        out = [_dot(stack(l_ak[g], a_rk[g]), bdiag(vv[g])) for g in groups]
        x_mid = [o[:WKV_CHUNK] for o in out]
        av = [o[WKV_CHUNK:] for o in out]
        w_mat = [_dot(t_inv[g], bdiag(at[g])) for g in groups]
        u0 = [_dot(t_inv[g], bdiag(x_mid[g])) for g in groups]
        q_hat = [rt[g].astype(F32) + _dot(a_rb[g], bdiag(w_mat[g])) for g in groups]
        y0 = [_dot(a_rb[g], bdiag(u0[g])) + av[g] for g in groups]
        st = [state_ref[sidx, g] for g in groups]
        out = [_dot_nt(stack(q_hat[g], w_mat[g]), st[g]) for g in groups]
        for g in groups:
            y_ref[rsl, lanes[g]] = out[g][:WKV_CHUNK] + y0[g]
        uv = [jnp.concatenate([out[g][WKV_CHUNK:] + u0[g], vv[g].astype(F32)], axis=0) for g in groups]
        bk = [stack(bh_ref[rsl, lanes[g]], kh_ref[rsl, lanes[g]]) for g in groups]
        upd = [_dot(uv[g].T, bk[g]) for g in groups]
        for g in groups:
            pc = pc_ref[pl.ds(pl.multiple_of(i * 8, 8), 8), lanes[g]][0:1]
            state_ref[sidx, g] = st[g] * pc + upd[g] * bd_f32
        return carry

    lax.fori_loop(0, n_chunks, chunk_body, 0)

    @pl.when(t == pl.num_programs(1) - 1)
    def _():
        for s in range(n_seq):
            for g in range(N_GROUPS):
                st = state_ref[s, g]
                acc = st[:, 0:HEAD_DIM]
                for j in range(1, HEADS_PER_GROUP):
                    acc = acc + st[:, j * HEAD_DIM:(j + 1) * HEAD_DIM]
                wkvo_ref[s, g] = acc

    y = y_ref[...]
    inv_hd = 1.0 / HEAD_DIM
    mean = _head_sum(y, ones_bf) * inv_hd
    yc = y - mean
    var = _head_sum(yc * yc, ones_bf) * inv_hd
    yn = yc * lax.rsqrt(var + GN_EPS) * gng_ref[...] + gnb_ref[...]
    o_a = (yn + bonus) * g_gate
    proj_a = _dot(o_a, wpa_ref[...])

    puv = jnp.dot(h, win_ref[:, SHIFT_W:SHIFT_W + 2 * GMLP_W], preferred_element_type=F32)
    pu = puv[:, 0:GMLP_W]
    pv = puv[:, GMLP_W:2 * GMLP_W]
    mu_v = jnp.mean(pv, axis=-1, keepdims=True)
    pvc = pv - mu_v
    var_v = jnp.mean(pvc * pvc, axis=-1, keepdims=True)
    vn = pvc * lax.rsqrt(var_v + LN_EPS) * lng_ref[...] + lnb_ref[...]
    if emit_vn:
        vno_ref[...] = vn.reshape(vno_ref.shape)
    vn_bf = vn.astype(BF16)
    sp_mask = (_block_id(ri, gmlp_chunk) == _block_id(ci, gmlp_chunk)) & (ci <= ri)
    zs = []
    for g in range(GMLP_GROUPS):
        ws_g = jnp.where(sp_mask, ws_ref[g], 0.0).astype(BF16)
        zs.append(jnp.dot(ws_g, vn_bf[:, g * GMLP_GC:(g + 1) * GMLP_GC], preferred_element_type=F32))
    z = jnp.concatenate(zs, axis=1) + jnp.concatenate([bias_ref[...]] * (rows // gmlp_chunk), axis=0)
    o_b = pu * z
    proj_b = _dot(o_b, wpb_ref[...])

    pg = jnp.dot(h, win_ref[:, SHIFT_W + 2 * GMLP_W:IN_W], preferred_element_type=F32)
    m = jax.nn.sigmoid(pg[:, 0:D_MODEL]) * proj_a + jax.nn.sigmoid(pg[:, D_MODEL:2 * D_MODEL]) * proj_b
    mo = _dot(m, wo_ref[...])
    xo_ref[...] = (x + _rms(mo, gpost_ref[...])).reshape(xo_ref.shape)


def _ffn_kernel(x_ref, gpre_ref, wgate_ref, wup_ref, wdown_ref, gpost_ref, o_ref):
    x = x_ref[...]
    f = _rms(x, gpre_ref[...]).astype(BF16)
    gate = jnp.dot(f, wgate_ref[...], preferred_element_type=F32)
    up = jnp.dot(f, wup_ref[...], preferred_element_type=F32)
    hid = (gate * jax.nn.sigmoid(gate) * up).astype(BF16)
    f2 = jnp.dot(hid, wdown_ref[...], preferred_element_type=F32)
    o_ref[...] = x + _rms(f2, gpost_ref[...])


def _resident(shape):
    nd = len(shape)
    return pl.BlockSpec(shape, lambda *_: (0,) * nd, pipeline_mode=pl.Buffered(1))


def _mix_call(x, shift0, wkv0_bd, params, *, n_seq, seq_rows, gmlp_chunk, emit_vn):
    batch, seq_len, _ = x.shape
    rows = n_seq * seq_rows
    assert rows == MIX_ROWS and batch % n_seq == 0 and seq_len % seq_rows == 0
    assert seq_rows % gmlp_chunk == 0 and seq_rows % WKV_CHUNK == 0
    grid = (batch // n_seq, seq_len // seq_rows)

    ws = jnp.tile(params['w_spatial'][:, :gmlp_chunk, :gmlp_chunk], (1, rows // gmlp_chunk, rows // gmlp_chunk))
    bias = jnp.repeat(params['b_spatial'][:, :gmlp_chunk].T, GMLP_GC, axis=1)
    weights = [params['g_pre_mix'], params['w_in'], params['mu_shift'], params['w0'], params['w_wa'], params['a0'],
               params['w_lora_g'], params['k_k'], params['k_a'], params['r_k'], params['gn_g'], params['gn_b'],
               params['ln_v_g'], params['ln_v_b'], ws, bias, params['w_proj_a'], params['w_proj_b'], params['w_o'],
               params['g_post_mix']]

    x_spec = pl.BlockSpec((n_seq, seq_rows, D_MODEL), lambda b, t: (b, t, 0))
    per_seq = lambda shape: pl.BlockSpec((n_seq,) + shape, lambda b, t: (b,) + (0,) * len(shape),
                                         pipeline_mode=pl.Buffered(1))
    in_specs = [x_spec, per_seq((1, SHIFT_W)), per_seq((N_GROUPS, MXU_TILE, HEAD_DIM))]
    in_specs += [_resident(w.shape) for w in weights]
    out_shape = [jax.ShapeDtypeStruct(x.shape, F32),
                 jax.ShapeDtypeStruct((batch, 1, SHIFT_W), F32),
                 jax.ShapeDtypeStruct((batch, N_GROUPS, MXU_TILE, HEAD_DIM), F32)]
    out_specs = [x_spec, per_seq((1, SHIFT_W)), per_seq((N_GROUPS, MXU_TILE, HEAD_DIM))]
    if emit_vn:
        out_shape.append(jax.ShapeDtypeStruct((batch, seq_len, GMLP_W), F32))
        out_specs.append(x_spec)
    act = lambda dt: pltpu.VMEM((rows, RWKV_W), dt)
    scratch = [pltpu.VMEM((n_seq, 1, SHIFT_W), F32),
               pltpu.VMEM((n_seq, N_GROUPS, MXU_TILE, MXU_TILE), F32),
               act(BF16), act(BF16), act(BF16), act(BF16), act(BF16), act(BF16), act(BF16),
               pltpu.VMEM((rows // WKV_CHUNK * 8, RWKV_W), F32), act(F32)]
    kern = functools.partial(_mix_kernel, n_seq=n_seq, seq_rows=seq_rows, gmlp_chunk=gmlp_chunk, emit_vn=emit_vn)
    return pl.pallas_call(
        kern, grid=grid, in_specs=in_specs, out_specs=out_specs, out_shape=out_shape, scratch_shapes=scratch,
        compiler_params=pltpu.CompilerParams(dimension_semantics=("arbitrary", "arbitrary"),
                                             vmem_limit_bytes=VMEM_LIMIT),
    )(x, shift0, wkv0_bd, *weights)


def _ffn_call(x2d, params):
    n = x2d.shape[0]
    assert n % FFN_ROWS == 0
    weights = [params['g_pre_ffn'], params['w_gate'], params['w_up'], params['w_down'], params['g_post_ffn']]
    row_spec = pl.BlockSpec((FFN_ROWS, D_MODEL), lambda i: (i, 0))
    return pl.pallas_call(
        _ffn_kernel, grid=(n // FFN_ROWS,), in_specs=[row_spec] + [_resident(w.shape) for w in weights],
        out_specs=row_spec, out_shape=jax.ShapeDtypeStruct(x2d.shape, F32),
        compiler_params=pltpu.CompilerParams(dimension_semantics=("arbitrary",), vmem_limit_bytes=VMEM_LIMIT),
    )(x2d, *weights)


def _layer(x, shift0, wkv0, params, *, n_seq, seq_rows, gmlp_chunk, emit_vn):
    wkv0_g = wkv0.astype(F32).reshape(wkv0.shape[0], N_GROUPS, MXU_TILE, HEAD_DIM)
    outs = _mix_call(x, shift0, wkv0_g, params, n_seq=n_seq, seq_rows=seq_rows,
                     gmlp_chunk=gmlp_chunk, emit_vn=emit_vn)
    x_mid, shift, wkv_g = outs[:3]
    y = _ffn_call(x_mid.reshape(-1, D_MODEL), params).reshape(x.shape)
    return y, shift, wkv_g.reshape(wkv0.shape), (outs[3] if emit_vn else None)


def kernel(x_prompt, x_sample, state_shift, state_wkv, g_pre_mix, w_in, mu_shift, w0, w_lora_w, a0, w_lora_a, w_lora_g, k_k, k_a, r_k, gn_g, gn_b, ln_v_g, ln_v_b, w_spatial, b_spatial, w_proj_a, w_proj_b, w_o, g_post_mix, g_pre_ffn, w_gate, w_up, w_down, g_post_ffn):
    depth = w_in.shape[0]
    yp, ys = x_prompt, x_sample
    bp = x_prompt.shape[0]
    shift_p, wkv_p, shift_s, wkv_s, v_s = [], [], [], [], []
    for l in range(depth):
        row = lambda a: a[l].reshape(1, -1).astype(F32)
        zero = jnp.zeros((LORA_W, RWKV_W), BF16)
        w_wa = jnp.concatenate(
            [jnp.concatenate([w_lora_w[l].astype(BF16), zero], axis=1),
             jnp.concatenate([zero, w_lora_a[l].astype(BF16)], axis=1)], axis=0)
        params = {
            'g_pre_mix': row(g_pre_mix), 'w_in': w_in[l].astype(BF16), 'mu_shift': row(mu_shift), 'w0': row(w0),
            'w_wa': w_wa, 'a0': row(a0), 'w_lora_g': w_lora_g[l].astype(BF16), 'k_k': row(k_k), 'k_a': row(k_a),
            'r_k': row(r_k), 'gn_g': row(gn_g), 'gn_b': row(gn_b), 'ln_v_g': row(ln_v_g), 'ln_v_b': row(ln_v_b),
            'w_spatial': w_spatial[l].astype(F32), 'b_spatial': b_spatial[l].astype(F32),
            'w_proj_a': w_proj_a[l].astype(BF16), 'w_proj_b': w_proj_b[l].astype(BF16), 'w_o': w_o[l].astype(BF16),
            'g_post_mix': row(g_post_mix), 'g_pre_ffn': row(g_pre_ffn), 'w_gate': w_gate[l].astype(BF16),
            'w_up': w_up[l].astype(BF16), 'w_down': w_down[l].astype(BF16), 'g_post_ffn': row(g_post_ffn),
        }
        zero_row = jnp.zeros((bp, 1, SHIFT_W), F32)
        zero_s = jnp.zeros((bp, N_HEADS, HEAD_DIM, HEAD_DIM), F32)
        yp, sp, wp, _ = _layer(yp, zero_row, zero_s, params, n_seq=1, seq_rows=MIX_ROWS,
                               gmlp_chunk=GMLP_CHUNK, emit_vn=False)
        dec_seq = x_sample.shape[1]
        ys, ss, wsm, vs = _layer(ys, state_shift[l], state_wkv[l], params, n_seq=MIX_ROWS // dec_seq,
                                 seq_rows=dec_seq, gmlp_chunk=dec_seq, emit_vn=True)
        shift_p.append(sp); wkv_p.append(wp); shift_s.append(ss); wkv_s.append(wsm); v_s.append(vs)
    return (yp, ys, jnp.stack(shift_p), jnp.stack(wkv_p), jnp.stack(shift_s), jnp.stack(wkv_s), jnp.stack(v_s))
```

```python
import functools

import jax
import jax.numpy as jnp
from jax import lax
from jax.experimental import pallas as pl
from jax.experimental.pallas import tpu as pltpu

F32 = jnp.float32
BF16 = jnp.bfloat16

D_MODEL = 1024
HEAD_DIM = 64
N_HEADS = D_MODEL // HEAD_DIM
RWKV_W = N_HEADS * HEAD_DIM
LORA_W = 64
LORA_A = 64
LORA_G = 128
SHIFT_W = 3 * RWKV_W + LORA_W + LORA_A + LORA_G
GMLP_W = D_MODEL
GMLP_GC = 128
GMLP_CHUNK = 128
GMLP_GROUPS = GMLP_W // GMLP_GC
IN_W = SHIFT_W + 2 * GMLP_W + 2 * D_MODEL
RMS_EPS = 1e-6
GN_EPS = 64e-5
LN_EPS = 1e-5

MXU_TILE = 256
WKV_CHUNK = 64
HEADS_PER_GROUP = MXU_TILE // HEAD_DIM
N_GROUPS = RWKV_W // MXU_TILE
MIX_ROWS = 256
FFN_ROWS = 512
VMEM_LIMIT = 60 * 1024 * 1024

_NEUMANN_STEPS = 4


def _dot(a, b):
    return jnp.dot(a.astype(BF16), b.astype(BF16), preferred_element_type=F32)


def _dot_nt(a, b):
    return lax.dot_general(a.astype(BF16), b.astype(BF16), (((1,), (1,)), ((), ())),
                           preferred_element_type=F32)


def _block_id(idx, block):
    assert block & (block - 1) == 0
    return lax.shift_right_logical(idx, block.bit_length() - 1)


def _rms(x, g):
    return x * lax.rsqrt(jnp.mean(x * x, axis=-1, keepdims=True) + RMS_EPS) * g


def _split2(x):
    hi = x.astype(BF16)
    lo = (x - hi.astype(F32)).astype(BF16)
    return hi, lo


def _head_sum(x, ones_bd):
    xb = x.astype(BF16)
    cols = [jnp.dot(xb[:, g * MXU_TILE:(g + 1) * MXU_TILE], ones_bd, preferred_element_type=F32)
            for g in range(N_GROUPS)]
    return jnp.concatenate(cols, axis=1)


def _mix_kernel(x_ref, shift0_ref, wkv0_ref, gpre_ref, win_ref, mu_ref, w0_ref, wwa_ref, a0_ref, wg_ref,
                kk_ref, ka_ref, rk_ref, gng_ref, gnb_ref, lng_ref, lnb_ref, ws_ref, bias_ref,
                wpa_ref, wpb_ref, wo_ref, gpost_ref,
                xo_ref, shifto_ref, wkvo_ref, *rest, n_seq, seq_rows, gmlp_chunk, emit_vn):
    if emit_vn:
        vno_ref, rest = rest[0], rest[1:]
    (carry_ref, state_ref, at_ref, bt_ref, kt_ref, rt_ref, vb_ref, bh_ref, kh_ref, pc_ref, y_ref) = rest
    rows = n_seq * seq_rows
    n_chunks = rows // WKV_CHUNK
    chunks_per_seq = seq_rows // WKV_CHUNK
    t = pl.program_id(1)

    same_head = (_block_id(lax.broadcasted_iota(jnp.int32, (MXU_TILE, MXU_TILE), 0), HEAD_DIM)
                 == _block_id(lax.broadcasted_iota(jnp.int32, (MXU_TILE, MXU_TILE), 1), HEAD_DIM))
    bd_f32 = jnp.where(same_head, 1.0, 0.0)
    bd_bf = bd_f32.astype(BF16)

    @pl.when(t == 0)
    def _():
        carry_ref[...] = shift0_ref[...]
        for s in range(n_seq):
            for g in range(N_GROUPS):
                c = wkv0_ref[s, g]
                state_ref[s, g] = jnp.concatenate([c] * HEADS_PER_GROUP, axis=1) * bd_f32

    x = x_ref[...].reshape(rows, D_MODEL)
    h = _rms(x, gpre_ref[...]).astype(BF16)

    ps = jnp.dot(h, win_ref[:, 0:SHIFT_W], preferred_element_type=F32)
    puv = jnp.dot(h, win_ref[:, SHIFT_W:SHIFT_W + 2 * GMLP_W], preferred_element_type=F32)

    row_id = lax.broadcasted_iota(jnp.int32, (rows, 1), 0)
    p_prev = pltpu.roll(ps, 1, 0)
    for s in range(n_seq):
        p_prev = jnp.where(row_id == s * seq_rows, carry_ref[s], p_prev)
        last = ps[(s + 1) * seq_rows - 1:(s + 1) * seq_rows, :]
        carry_ref[s] = last
        shifto_ref[s] = last
    xm = ps + (p_prev - ps) * mu_ref[...]
    r = xm[:, 0:RWKV_W]
    k = xm[:, RWKV_W:2 * RWKV_W]
    v = xm[:, 2 * RWKV_W:3 * RWKV_W]
    o = 3 * RWKV_W
    dwa = xm[:, o:o + LORA_W + LORA_A]
    dg = xm[:, o + LORA_W + LORA_A:SHIFT_W]
    lane = lax.broadcasted_iota(jnp.int32, (1, LORA_W + LORA_A), 1)
    dwa = jnp.where(lane < LORA_W, jnp.tanh(dwa), dwa)
    lwa = _dot(dwa, wwa_ref[...])
    zw = w0_ref[...] + lwa[:, 0:RWKV_W]
    w_log = -(jnp.maximum(-zw, 0.0) + jnp.log1p(jnp.exp(-jnp.abs(zw)))) - 0.5
    lw = -jnp.exp(w_log)
    a_rate = jax.nn.sigmoid(a0_ref[...] + lwa[:, RWKV_W:2 * RWKV_W])
    g_gate = _dot(jax.nn.sigmoid(dg), wg_ref[...])

    kk = k * kk_ref[...]
    kk = kk * lax.rsqrt(jnp.maximum(_head_sum(kk * kk, bd_bf), 1e-24))
    k_mod = k * (1.0 + (a_rate - 1.0) * ka_ref[...])
    bonus = _head_sum(r * k_mod * rk_ref[...], bd_bf) * v

    ri = lax.broadcasted_iota(jnp.int32, (rows, rows), 0)
    ci = lax.broadcasted_iota(jnp.int32, (rows, rows), 1)
    same_chunk = _block_id(ri, WKV_CHUNK) == _block_id(ci, WKV_CHUNK)
    tri = jnp.where(same_chunk & (ci <= ri), 1.0, 0.0).astype(BF16)
    lw_hi, lw_lo = _split2(lw)
    cs = (jnp.dot(tri, lw_hi, preferred_element_type=F32)
          + jnp.dot(tri, lw_lo, preferred_element_type=F32))
    ends = []
    for c in range(n_chunks):
        cs_end = cs[(c + 1) * WKV_CHUNK - 1:(c + 1) * WKV_CHUNK, :]
        pc_ref[c * 8:(c + 1) * 8, :] = jnp.broadcast_to(jnp.exp(cs_end), (8, RWKV_W))
        ends.append(jnp.broadcast_to(cs_end, (WKV_CHUNK, RWKV_W)))
    to_end = jnp.exp(jnp.concatenate(ends, axis=0) - cs)
    p_inv = jnp.exp(-cs)
    at_ref[...] = (-kk * jnp.exp(cs - lw)).astype(BF16)
    b_vec = kk * a_rate
    bt_ref[...] = (b_vec * p_inv).astype(BF16)
    kt_ref[...] = (k_mod * p_inv).astype(BF16)
    rt_ref[...] = (r * jnp.exp(cs)).astype(BF16)
    vb_ref[...] = v.astype(BF16)
    bh_ref[...] = (b_vec * to_end).astype(BF16)
    kh_ref[...] = (k_mod * to_end).astype(BF16)

    pu = puv[:, 0:GMLP_W]
    pv = puv[:, GMLP_W:2 * GMLP_W]
    mu_v = jnp.mean(pv, axis=-1, keepdims=True)
    pvc = pv - mu_v
    var_v = jnp.mean(pvc * pvc, axis=-1, keepdims=True)
    vn = pvc * lax.rsqrt(var_v + LN_EPS) * lng_ref[...] + lnb_ref[...]
    if emit_vn:
        vno_ref[...] = vn.reshape(vno_ref.shape)
    vn_bf = vn.astype(BF16)
    sp_mask = (_block_id(ri, gmlp_chunk) == _block_id(ci, gmlp_chunk)) & (ci <= ri)
    col = lambda j: slice(j * MXU_TILE, (j + 1) * MXU_TILE)
    zs = [None] * GMLP_GROUPS
    projb_cols = [None] * (D_MODEL // MXU_TILE)
    pg_cols = [None] * (2 * D_MODEL // MXU_TILE)
    ob_box = []

    def spatial(g):
        ws_g = jnp.where(sp_mask, ws_ref[g], 0.0).astype(BF16)
        zs[g] = jnp.dot(ws_g, vn_bf[:, g * GMLP_GC:(g + 1) * GMLP_GC], preferred_element_type=F32)

    def proj_b_col(j):
        if not ob_box:
            z = jnp.concatenate(zs, axis=1) + jnp.concatenate([bias_ref[...]] * (rows // gmlp_chunk), axis=0)
            ob_box.append((pu * z).astype(BF16))
        projb_cols[j] = jnp.dot(ob_box[0], wpb_ref[:, col(j)], preferred_element_type=F32)

    def gate_col(j):
        off = SHIFT_W + 2 * GMLP_W
        pg_cols[j] = jnp.dot(h, win_ref[:, off + j * MXU_TILE:off + (j + 1) * MXU_TILE],
                             preferred_element_type=F32)

    fillers = ([functools.partial(spatial, g) for g in range(GMLP_GROUPS)]
               + [functools.partial(proj_b_col, j) for j in range(len(projb_cols))]
               + [functools.partial(gate_col, j) for j in range(len(pg_cols))])
    fillers.reverse()

    def fill(n=1):
        for _ in range(n):
            if fillers:
                fillers.pop()()

    tq = lax.broadcasted_iota(jnp.int32, (WKV_CHUNK, MXU_TILE), 0)
    sq = lax.broadcasted_iota(jnp.int32, (WKV_CHUNK, MXU_TILE), 1) & (WKV_CHUNK - 1)
    strict = sq < tq
    incl = sq <= tq
    eye_all = jnp.where(sq == tq, 1.0, 0.0)

    def bdiag(m):
        m = m.astype(BF16)
        return jnp.concatenate([m] * HEADS_PER_GROUP, axis=0) * bd_bf

    groups = range(N_GROUPS)
    lanes = [col(g) for g in groups]
    stack = lambda top_, bottom_: jnp.concatenate([top_.astype(BF16), bottom_.astype(BF16)], axis=0)
    top = lambda m: m[:WKV_CHUNK]
    bottom = lambda m: m[WKV_CHUNK:]
    row_sl = [slice(c * WKV_CHUNK, (c + 1) * WKV_CHUNK) for c in range(n_chunks)]
    chains = [(c, g) for c in range(n_chunks) for g in groups]
    load = lambda ref: [ref[row_sl[c], lanes[g]] for c, g in chains]
    each = lambda fn, *lists: [fn(*vals) for vals in zip(*lists)]
    at, rt, vv = load(at_ref), load(rt_ref), load(vb_ref)
    lhs = each(stack, at, rt)
    s1 = each(lambda l, m: _dot_nt(l, bdiag(m)), lhs, load(bt_ref))
    fill()
    s2 = each(lambda l, m: _dot_nt(l, bdiag(m)), lhs, load(kt_ref))
    fill()
    l_ab = [jnp.where(strict, top(s), 0.0) for s in s1]
    a_rb = [jnp.where(incl, bottom(s), 0.0).astype(BF16) for s in s1]
    l_ak = [jnp.where(strict, top(s), 0.0) for s in s2]
    a_rk = [jnp.where(incl, bottom(s), 0.0) for s in s2]
    l_pow = [_dot(l, bdiag(l)) for l in l_ab]
    fill()
    t_inv = [eye_all + l for l in l_ab]
    for _ in range(_NEUMANN_STEPS):
        out = each(lambda t_, p_: _dot(stack(t_, p_), bdiag(p_)), t_inv, l_pow)
        fill()
        t_inv = each(lambda t_, o_: t_ + top(o_), t_inv, out)
        l_pow = [bottom(o_) for o_ in out]
    t_inv = each(lambda t_, p_: (t_ + _dot(t_, bdiag(p_))).astype(BF16), t_inv, l_pow)
    fill()
    out = each(lambda a_, b_, v_: _dot(stack(a_, b_), bdiag(v_)), l_ak, a_rk, vv)
    fill()
    x_mid = [top(o_) for o_ in out]
    av = [bottom(o_) for o_ in out]
    w_mat = each(lambda t_, a_: _dot(t_, bdiag(a_)), t_inv, at)
    fill()
    u0 = each(lambda t_, x_: _dot(t_, bdiag(x_)), t_inv, x_mid)
    fill()
    q_hat = each(lambda r_, a_, w_: r_.astype(F32) + _dot(a_, bdiag(w_)), rt, a_rb, w_mat)
    fill()
    y0 = each(lambda a_, u_, v_: _dot(a_, bdiag(u_)) + v_, a_rb, u0, av)
    fill()
    qw = each(stack, q_hat, w_mat)
    bk = each(stack, load(bh_ref), load(kh_ref))
    st = None
    for c in range(n_chunks):
        s = c // chunks_per_seq
        mine = [n for n, (cc, _) in enumerate(chains) if cc == c]
        if c % chunks_per_seq == 0:
            st = [state_ref[s, g] for g in groups]
        out = [_dot_nt(qw[n], st[g]) for g, n in zip(groups, mine)]
        fill()
        for g, n in zip(groups, mine):
            y_ref[row_sl[c], lanes[g]] = top(out[g]) + y0[n]
        uv = [jnp.concatenate([bottom(out[g]) + u0[n], vv[n].astype(F32)], axis=0)
              for g, n in zip(groups, mine)]
        upd = [_dot(uv[g].T, bk[n]) for g, n in zip(groups, mine)]
        fill()
        st = [st[g] * pc_ref[c * 8:c * 8 + 1, lanes[g]] + upd[g] * bd_f32 for g in groups]
        if (c + 1) % chunks_per_seq == 0:
            for g in groups:
                state_ref[s, g] = st[g]
    fill(len(fillers))

    @pl.when(t == pl.num_programs(1) - 1)
    def _():
        for s in range(n_seq):
            for g in range(N_GROUPS):
                sg = state_ref[s, g]
                acc = sg[:, 0:HEAD_DIM]
                for j in range(1, HEADS_PER_GROUP):
                    acc = acc + sg[:, j * HEAD_DIM:(j + 1) * HEAD_DIM]
                wkvo_ref[s, g] = acc

    y = y_ref[...]
    inv_hd = 1.0 / HEAD_DIM
    mean = _head_sum(y, bd_bf) * inv_hd
    yc = y - mean
    var = _head_sum(yc * yc, bd_bf) * inv_hd
    yn = yc * lax.rsqrt(var + GN_EPS) * gng_ref[...] + gnb_ref[...]
    o_a = (yn + bonus) * g_gate
    proj_a = _dot(o_a, wpa_ref[...])

    proj_b = jnp.concatenate(projb_cols, axis=1)
    pg = jnp.concatenate(pg_cols, axis=1)
    m = jax.nn.sigmoid(pg[:, 0:D_MODEL]) * proj_a + jax.nn.sigmoid(pg[:, D_MODEL:2 * D_MODEL]) * proj_b
    mo = _dot(m, wo_ref[...])
    xo_ref[...] = (x + _rms(mo, gpost_ref[...])).reshape(xo_ref.shape)


def _ffn_kernel(x_ref, gpre_ref, wgate_ref, wup_ref, wdown_ref, gpost_ref, o_ref):
    x = x_ref[...]
    f = _rms(x, gpre_ref[...]).astype(BF16)
    gate = jnp.dot(f, wgate_ref[...], preferred_element_type=F32)
    up = jnp.dot(f, wup_ref[...], preferred_element_type=F32)
    hid = (gate * jax.nn.sigmoid(gate) * up).astype(BF16)
    f2 = jnp.dot(hid, wdown_ref[...], preferred_element_type=F32)
    o_ref[...] = x + _rms(f2, gpost_ref[...])


def _resident(shape):
    nd = len(shape)
    return pl.BlockSpec(shape, lambda *_: (0,) * nd, pipeline_mode=pl.Buffered(1))


def _mix_call(x, shift0, wkv0_bd, params, *, n_seq, seq_rows, gmlp_chunk, emit_vn):
    batch, seq_len, _ = x.shape
    rows = n_seq * seq_rows
    assert rows == MIX_ROWS and batch % n_seq == 0 and seq_len % seq_rows == 0
    assert seq_rows % gmlp_chunk == 0 and seq_rows % WKV_CHUNK == 0
    grid = (batch // n_seq, seq_len // seq_rows)

    ws = jnp.tile(params['w_spatial'][:, :gmlp_chunk, :gmlp_chunk], (1, rows // gmlp_chunk, rows // gmlp_chunk))
    bias = jnp.repeat(params['b_spatial'][:, :gmlp_chunk].T, GMLP_GC, axis=1)
    weights = [params['g_pre_mix'], params['w_in'], params['mu_shift'], params['w0'], params['w_wa'], params['a0'],
               params['w_lora_g'], params['k_k'], params['k_a'], params['r_k'], params['gn_g'], params['gn_b'],
               params['ln_v_g'], params['ln_v_b'], ws, bias, params['w_proj_a'], params['w_proj_b'], params['w_o'],
               params['g_post_mix']]

    x_spec = pl.BlockSpec((n_seq, seq_rows, D_MODEL), lambda b, t: (b, t, 0))
    per_seq = lambda shape: pl.BlockSpec((n_seq,) + shape, lambda b, t: (b,) + (0,) * len(shape),
                                         pipeline_mode=pl.Buffered(1))
    in_specs = [x_spec, per_seq((1, SHIFT_W)), per_seq((N_GROUPS, MXU_TILE, HEAD_DIM))]
    in_specs += [_resident(w.shape) for w in weights]
    out_shape = [jax.ShapeDtypeStruct(x.shape, F32),
                 jax.ShapeDtypeStruct((batch, 1, SHIFT_W), F32),
                 jax.ShapeDtypeStruct((batch, N_GROUPS, MXU_TILE, HEAD_DIM), F32)]
    out_specs = [x_spec, per_seq((1, SHIFT_W)), per_seq((N_GROUPS, MXU_TILE, HEAD_DIM))]
    if emit_vn:
        out_shape.append(jax.ShapeDtypeStruct((batch, seq_len, GMLP_W), F32))
        out_specs.append(x_spec)
    act = lambda dt: pltpu.VMEM((rows, RWKV_W), dt)
    scratch = [pltpu.VMEM((n_seq, 1, SHIFT_W), F32),
               pltpu.VMEM((n_seq, N_GROUPS, MXU_TILE, MXU_TILE), F32),
               act(BF16), act(BF16), act(BF16), act(BF16), act(BF16), act(BF16), act(BF16),
               pltpu.VMEM((rows // WKV_CHUNK * 8, RWKV_W), F32), act(F32)]
    kern = functools.partial(_mix_kernel, n_seq=n_seq, seq_rows=seq_rows, gmlp_chunk=gmlp_chunk, emit_vn=emit_vn)
    return pl.pallas_call(
        kern, grid=grid, in_specs=in_specs, out_specs=out_specs, out_shape=out_shape, scratch_shapes=scratch,
        compiler_params=pltpu.CompilerParams(dimension_semantics=("arbitrary", "arbitrary"),
                                             vmem_limit_bytes=VMEM_LIMIT),
    )(x, shift0, wkv0_bd, *weights)


def _ffn_call(x2d, params):
    n = x2d.shape[0]
    assert n % FFN_ROWS == 0
    weights = [params['g_pre_ffn'], params['w_gate'], params['w_up'], params['w_down'], params['g_post_ffn']]
    row_spec = pl.BlockSpec((FFN_ROWS, D_MODEL), lambda i: (i, 0))
    return pl.pallas_call(
        _ffn_kernel, grid=(n // FFN_ROWS,), in_specs=[row_spec] + [_resident(w.shape) for w in weights],
        out_specs=row_spec, out_shape=jax.ShapeDtypeStruct(x2d.shape, F32),
        compiler_params=pltpu.CompilerParams(dimension_semantics=("arbitrary",), vmem_limit_bytes=VMEM_LIMIT),
    )(x2d, *weights)


def _layer(x, shift0, wkv0, params, *, n_seq, seq_rows, gmlp_chunk, emit_vn):
    wkv0_g = wkv0.astype(F32).reshape(wkv0.shape[0], N_GROUPS, MXU_TILE, HEAD_DIM)
    outs = _mix_call(x, shift0, wkv0_g, params, n_seq=n_seq, seq_rows=seq_rows,
                     gmlp_chunk=gmlp_chunk, emit_vn=emit_vn)
    x_mid, shift, wkv_g = outs[:3]
    y = _ffn_call(x_mid.reshape(-1, D_MODEL), params).reshape(x.shape)
    return y, shift, wkv_g.reshape(wkv0.shape), (outs[3] if emit_vn else None)


def kernel(x_prompt, x_sample, state_shift, state_wkv, g_pre_mix, w_in, mu_shift, w0, w_lora_w, a0, w_lora_a, w_lora_g, k_k, k_a, r_k, gn_g, gn_b, ln_v_g, ln_v_b, w_spatial, b_spatial, w_proj_a, w_proj_b, w_o, g_post_mix, g_pre_ffn, w_gate, w_up, w_down, g_post_ffn):
    depth = w_in.shape[0]
    yp, ys = x_prompt, x_sample
    bp = x_prompt.shape[0]
    shift_p, wkv_p, shift_s, wkv_s, v_s = [], [], [], [], []
    for l in range(depth):
        row = lambda a: a[l].reshape(1, -1).astype(F32)
        zero = jnp.zeros((LORA_W, RWKV_W), BF16)
        w_wa = jnp.concatenate(
            [jnp.concatenate([w_lora_w[l].astype(BF16), zero], axis=1),
             jnp.concatenate([zero, w_lora_a[l].astype(BF16)], axis=1)], axis=0)
        params = {
            'g_pre_mix': row(g_pre_mix), 'w_in': w_in[l].astype(BF16), 'mu_shift': row(mu_shift), 'w0': row(w0),
            'w_wa': w_wa, 'a0': row(a0), 'w_lora_g': w_lora_g[l].astype(BF16), 'k_k': row(k_k), 'k_a': row(k_a),
            'r_k': row(r_k), 'gn_g': row(gn_g), 'gn_b': row(gn_b), 'ln_v_g': row(ln_v_g), 'ln_v_b': row(ln_v_b),
            'w_spatial': w_spatial[l].astype(F32), 'b_spatial': b_spatial[l].astype(F32),
            'w_proj_a': w_proj_a[l].astype(BF16), 'w_proj_b': w_proj_b[l].astype(BF16), 'w_o': w_o[l].astype(BF16),
            'g_post_mix': row(g_post_mix), 'g_pre_ffn': row(g_pre_ffn), 'w_gate': w_gate[l].astype(BF16),
            'w_up': w_up[l].astype(BF16), 'w_down': w_down[l].astype(BF16), 'g_post_ffn': row(g_post_ffn),
        }
        zero_row = jnp.zeros((bp, 1, SHIFT_W), F32)
        zero_s = jnp.zeros((bp, N_HEADS, HEAD_DIM, HEAD_DIM), F32)
        yp, sp, wp, _ = _layer(yp, zero_row, zero_s, params, n_seq=1, seq_rows=MIX_ROWS,
                               gmlp_chunk=GMLP_CHUNK, emit_vn=False)
        dec_seq = x_sample.shape[1]
        ys, ss, wsm, vs = _layer(ys, state_shift[l], state_wkv[l], params, n_seq=MIX_ROWS // dec_seq,
                                 seq_rows=dec_seq, gmlp_chunk=dec_seq, emit_vn=True)
        shift_p.append(sp); wkv_p.append(wp); shift_s.append(ss); wkv_s.append(wsm); v_s.append(vs)
    return (yp, ys, jnp.stack(shift_p), jnp.stack(wkv_p), jnp.stack(shift_s), jnp.stack(wkv_s), jnp.stack(v_s))
```

```python
import functools

import jax
import jax.numpy as jnp
from jax import lax
from jax.experimental import pallas as pl
from jax.experimental.pallas import tpu as pltpu

F32 = jnp.float32
BF16 = jnp.bfloat16

D_MODEL = 1024
HEAD_DIM = 64
N_HEADS = D_MODEL // HEAD_DIM
RWKV_W = N_HEADS * HEAD_DIM
LORA_W = 64
LORA_A = 64
LORA_G = 128
SHIFT_W = 3 * RWKV_W + LORA_W + LORA_A + LORA_G
GMLP_W = D_MODEL
GMLP_GC = 128
GMLP_CHUNK = 128
GMLP_GROUPS = GMLP_W // GMLP_GC
IN_W = SHIFT_W + 2 * GMLP_W + 2 * D_MODEL
RMS_EPS = 1e-6
GN_EPS = 64e-5
LN_EPS = 1e-5
EXP_NEG_HALF = 0.6065306597126334

MXU_TILE = 256
WKV_CHUNK = 64
HEADS_PER_GROUP = MXU_TILE // HEAD_DIM
N_GROUPS = RWKV_W // MXU_TILE
MIX_ROWS = 256
FFN_ROWS = 512
VMEM_LIMIT = 60 * 1024 * 1024

_NEUMANN_STEPS = 4


def _dot(a, b):
    return jnp.dot(a.astype(BF16), b.astype(BF16), preferred_element_type=F32)


def _dot_nt(a, b):
    return lax.dot_general(a.astype(BF16), b.astype(BF16), (((1,), (1,)), ((), ())),
                           preferred_element_type=F32)


def _block_id(idx, block):
    assert block & (block - 1) == 0
    return lax.shift_right_logical(idx, block.bit_length() - 1)


def _rms(x, g):
    return x * lax.rsqrt(jnp.mean(x * x, axis=-1, keepdims=True) + RMS_EPS) * g


def _split2(x):
    hi = x.astype(BF16)
    lo = (x - hi.astype(F32)).astype(BF16)
    return hi, lo


def _head_sum(x, ones_bd):
    xb = x.astype(BF16)
    cols = [jnp.dot(xb[:, g * MXU_TILE:(g + 1) * MXU_TILE], ones_bd, preferred_element_type=F32)
            for g in range(N_GROUPS)]
    return jnp.concatenate(cols, axis=1)


def _mix_kernel(x_ref, shift0_ref, wkv0_ref, gpre_ref, win_ref, mu_ref, w0_ref, wwa_ref, a0_ref, wg_ref,
                kk_ref, ka_ref, rk_ref, gng_ref, gnb_ref, lng_ref, lnb_ref, ws_ref, bias_ref,
                wpa_ref, wpb_ref, wo_ref, gpost_ref,
                xo_ref, shifto_ref, wkvo_ref, *rest, n_seq, seq_rows, gmlp_chunk, emit_vn):
    if emit_vn:
        vno_ref, rest = rest[0], rest[1:]
    (carry_ref, state_ref, at_ref, bt_ref, kt_ref, rt_ref, vb_ref, bh_ref, kh_ref, pc_ref, y_ref) = rest
    rows = n_seq * seq_rows
    n_chunks = rows // WKV_CHUNK
    chunks_per_seq = seq_rows // WKV_CHUNK
    t = pl.program_id(1)

    same_head = (_block_id(lax.broadcasted_iota(jnp.int32, (MXU_TILE, MXU_TILE), 0), HEAD_DIM)
                 == _block_id(lax.broadcasted_iota(jnp.int32, (MXU_TILE, MXU_TILE), 1), HEAD_DIM))
    bd_f32 = jnp.where(same_head, 1.0, 0.0)
    bd_bf = bd_f32.astype(BF16)

    @pl.when(t == 0)
    def _():
        carry_ref[...] = shift0_ref[...]
        for s in range(n_seq):
            for g in range(N_GROUPS):
                c = wkv0_ref[s, g]
                state_ref[s, g] = jnp.concatenate([c] * HEADS_PER_GROUP, axis=1) * bd_f32

    x = x_ref[...].reshape(rows, D_MODEL)
    h = _rms(x, gpre_ref[...]).astype(BF16)

    ps = jnp.dot(h, win_ref[:, 0:SHIFT_W], preferred_element_type=F32)
    puv = jnp.dot(h, win_ref[:, SHIFT_W:SHIFT_W + 2 * GMLP_W], preferred_element_type=F32)

    row_id = lax.broadcasted_iota(jnp.int32, (rows, 1), 0)
    p_prev = pltpu.roll(ps, 1, 0)
    for s in range(n_seq):
        p_prev = jnp.where(row_id == s * seq_rows, carry_ref[s], p_prev)
        last = ps[(s + 1) * seq_rows - 1:(s + 1) * seq_rows, :]
        carry_ref[s] = last
        shifto_ref[s] = last
    xm = ps + (p_prev - ps) * mu_ref[...]
    r = xm[:, 0:RWKV_W]
    k = xm[:, RWKV_W:2 * RWKV_W]
    v = xm[:, 2 * RWKV_W:3 * RWKV_W]
    o = 3 * RWKV_W
    dwa = xm[:, o:o + LORA_W + LORA_A]
    dg = xm[:, o + LORA_W + LORA_A:SHIFT_W]
    lane = lax.broadcasted_iota(jnp.int32, (1, LORA_W + LORA_A), 1)
    dwa = jnp.where(lane < LORA_W, jnp.tanh(dwa), dwa)
    lwa = _dot(dwa, wwa_ref[...])
    zw = w0_ref[...] + lwa[:, 0:RWKV_W]
    lw = -EXP_NEG_HALF * jax.nn.sigmoid(zw)
    a_rate = jax.nn.sigmoid(a0_ref[...] + lwa[:, RWKV_W:2 * RWKV_W])
    g_gate = _dot(jax.nn.sigmoid(dg), wg_ref[...])

    kk = k * kk_ref[...]
    kk = kk * lax.rsqrt(jnp.maximum(_head_sum(kk * kk, bd_bf), 1e-24))
    k_mod = k * (1.0 + (a_rate - 1.0) * ka_ref[...])
    bonus = _head_sum(r * k_mod * rk_ref[...], bd_bf) * v

    ri = lax.broadcasted_iota(jnp.int32, (rows, rows), 0)
    ci = lax.broadcasted_iota(jnp.int32, (rows, rows), 1)
    same_chunk = _block_id(ri, WKV_CHUNK) == _block_id(ci, WKV_CHUNK)
    tri = jnp.where(same_chunk & (ci <= ri), 1.0, 0.0).astype(BF16)
    lw_hi, lw_lo = _split2(lw)
    cs = (jnp.dot(tri, lw_hi, preferred_element_type=F32)
          + jnp.dot(tri, lw_lo, preferred_element_type=F32))
    ends = []
    for c in range(n_chunks):
        cs_end = cs[(c + 1) * WKV_CHUNK - 1:(c + 1) * WKV_CHUNK, :]
        pc_ref[c * 8:(c + 1) * 8, :] = jnp.broadcast_to(jnp.exp(cs_end), (8, RWKV_W))
        ends.append(jnp.broadcast_to(cs_end, (WKV_CHUNK, RWKV_W)))
    to_end = jnp.exp(jnp.concatenate(ends, axis=0) - cs)
    p_inv = jnp.exp(-cs)
    at_ref[...] = (-kk * jnp.exp(cs - lw)).astype(BF16)
    b_vec = kk * a_rate
    bt_ref[...] = (b_vec * p_inv).astype(BF16)
    kt_ref[...] = (k_mod * p_inv).astype(BF16)
    rt_ref[...] = (r * jnp.exp(cs)).astype(BF16)
    vb_ref[...] = v.astype(BF16)
    bh_ref[...] = (b_vec * to_end).astype(BF16)
    kh_ref[...] = (k_mod * to_end).astype(BF16)

    pu = puv[:, 0:GMLP_W]
    pv = puv[:, GMLP_W:2 * GMLP_W]
    mu_v = jnp.mean(pv, axis=-1, keepdims=True)
    pvc = pv - mu_v
    var_v = jnp.mean(pvc * pvc, axis=-1, keepdims=True)
    vn = pvc * lax.rsqrt(var_v + LN_EPS) * lng_ref[...] + lnb_ref[...]
    if emit_vn:
        vno_ref[...] = vn.reshape(vno_ref.shape)
    vn_bf = vn.astype(BF16)
    sp_mask = (_block_id(ri, gmlp_chunk) == _block_id(ci, gmlp_chunk)) & (ci <= ri)
    col = lambda j: slice(j * MXU_TILE, (j + 1) * MXU_TILE)
    zs = [None] * GMLP_GROUPS
    projb_cols = [None] * (D_MODEL // MXU_TILE)
    pg_cols = [None] * (2 * D_MODEL // MXU_TILE)
    ob_box = []

    def spatial(g):
        ws_g = jnp.where(sp_mask, ws_ref[g], 0.0).astype(BF16)
        zs[g] = jnp.dot(ws_g, vn_bf[:, g * GMLP_GC:(g + 1) * GMLP_GC], preferred_element_type=F32)

    def proj_b_col(j):
        if not ob_box:
            z = jnp.concatenate(zs, axis=1) + jnp.concatenate([bias_ref[...]] * (rows // gmlp_chunk), axis=0)
            ob_box.append((pu * z).astype(BF16))
        projb_cols[j] = jnp.dot(ob_box[0], wpb_ref[:, col(j)], preferred_element_type=F32)

    def gate_col(j):
        off = SHIFT_W + 2 * GMLP_W
        pg_cols[j] = jnp.dot(h, win_ref[:, off + j * MXU_TILE:off + (j + 1) * MXU_TILE],
                             preferred_element_type=F32)

    fillers = ([functools.partial(spatial, g) for g in range(GMLP_GROUPS)]
               + [functools.partial(proj_b_col, j) for j in range(len(projb_cols))]
               + [functools.partial(gate_col, j) for j in range(len(pg_cols))])
    fillers.reverse()

    def fill(n=1):
        for _ in range(n):
            if fillers:
                fillers.pop()()

    per_state_stage = -(-len(fillers) // (2 * n_chunks))

    tq = lax.broadcasted_iota(jnp.int32, (WKV_CHUNK, MXU_TILE), 0)
    sq = lax.broadcasted_iota(jnp.int32, (WKV_CHUNK, MXU_TILE), 1) & (WKV_CHUNK - 1)
    strict = sq < tq
    incl = sq <= tq
    eye_all = jnp.where(sq == tq, 1.0, 0.0)

    def bdiag(m):
        m = m.astype(BF16)
        return jnp.concatenate([m] * HEADS_PER_GROUP, axis=0) * bd_bf

    groups = range(N_GROUPS)
    lanes = [col(g) for g in groups]
    stack = lambda top_, bottom_: jnp.concatenate([top_.astype(BF16), bottom_.astype(BF16)], axis=0)
    top = lambda m: m[:WKV_CHUNK]
    bottom = lambda m: m[WKV_CHUNK:]
    row_sl = [slice(c * WKV_CHUNK, (c + 1) * WKV_CHUNK) for c in range(n_chunks)]
    chains = [(c, g) for c in range(n_chunks) for g in groups]
    load = lambda ref: [ref[row_sl[c], lanes[g]] for c, g in chains]
    each = lambda fn, *lists: [fn(*vals) for vals in zip(*lists)]
    at, rt, vv = load(at_ref), load(rt_ref), load(vb_ref)
    lhs = each(stack, at, rt)
    s1 = each(lambda l, m: _dot_nt(l, bdiag(m)), lhs, load(bt_ref))
    s2 = each(lambda l, m: _dot_nt(l, bdiag(m)), lhs, load(kt_ref))
    l_ab = [jnp.where(strict, top(s), 0.0) for s in s1]
    a_rb = [jnp.where(incl, bottom(s), 0.0) for s in s1]
    l_ak = [jnp.where(strict, top(s), 0.0) for s in s2]
    a_rk = [jnp.where(incl, bottom(s), 0.0) for s in s2]
    l_pow = [_dot(l, bdiag(l)) for l in l_ab]
    t_inv = [eye_all + l for l in l_ab]
    for _ in range(_NEUMANN_STEPS):
        out = each(lambda t_, p_: _dot(stack(t_, p_), bdiag(p_)), t_inv, l_pow)
        t_inv = each(lambda t_, o_: t_ + top(o_), t_inv, out)
        l_pow = [bottom(o_) for o_ in out]
    t_inv = each(lambda t_, p_: t_ + _dot(t_, bdiag(p_)), t_inv, l_pow)
    out = each(lambda a_, b_, v_: _dot(stack(a_, b_), bdiag(v_)), l_ak, a_rk, vv)
    x_mid = [top(o_) for o_ in out]
    av = [bottom(o_) for o_ in out]
    tg = each(lambda t_, a_: stack(t_, _dot(a_, bdiag(t_))), t_inv, a_rb)
    out = each(lambda tg_, a_: _dot(tg_, bdiag(a_)), tg, at)
    w_mat = [top(o_) for o_ in out]
    q_hat = each(lambda r__, o_: r__.astype(F32) + bottom(o_), rt, out)
    out = each(lambda tg_, x_: _dot(tg_, bdiag(x_)), tg, x_mid)
    u0 = [top(o_) for o_ in out]
    y0 = each(lambda o_, v_: bottom(o_) + v_, out, av)
    qw = each(stack, q_hat, w_mat)
    bk = each(stack, load(bh_ref), load(kh_ref))
    st = None
    for c in range(n_chunks):
        s = c // chunks_per_seq
        mine = [n for n, (cc, _) in enumerate(chains) if cc == c]
        if c % chunks_per_seq == 0:
            st = [state_ref[s, g] for g in groups]
        out = [_dot_nt(qw[n], st[g]) for g, n in zip(groups, mine)]
        fill(per_state_stage)
        for g, n in zip(groups, mine):
            y_ref[row_sl[c], lanes[g]] = top(out[g]) + y0[n]
        uv = [jnp.concatenate([bottom(out[g]) + u0[n], vv[n].astype(F32)], axis=0)
              for g, n in zip(groups, mine)]
        upd = [_dot(uv[g].T, bk[n]) for g, n in zip(groups, mine)]
        fill(per_state_stage)
        st = [st[g] * pc_ref[c * 8:c * 8 + 1, lanes[g]] + upd[g] * bd_f32 for g in groups]
        if (c + 1) % chunks_per_seq == 0:
            for g in groups:
                state_ref[s, g] = st[g]
    fill(len(fillers))

    proj_b = jnp.concatenate(projb_cols, axis=1)
    pg = jnp.concatenate(pg_cols, axis=1)
    halves = [slice(0, rows // 2), slice(rows // 2, rows)]
    inv_hd = 1.0 / HEAD_DIM
    y = [y_ref[hs, :] for hs in halves]
    mean = [_head_sum(y_, bd_bf) * inv_hd for y_ in y]
    yc = [y_ - m_ for y_, m_ in zip(y, mean)]
    var = [_head_sum(yc_ * yc_, bd_bf) * inv_hd for yc_ in yc]
    o_a = [(yc_ * lax.rsqrt(var_ + GN_EPS) * gng_ref[...] + gnb_ref[...] + bonus[hs, :]) * g_gate[hs, :]
           for yc_, var_, hs in zip(yc, var, halves)]
    proj_a = [_dot(o_, wpa_ref[...]) for o_ in o_a]
    m = [jax.nn.sigmoid(pg[hs, 0:D_MODEL]) * pa_ + jax.nn.sigmoid(pg[hs, D_MODEL:2 * D_MODEL]) * proj_b[hs, :]
         for pa_, hs in zip(proj_a, halves)]
    mo = [_dot(m_, wo_ref[...]) for m_ in m]
    x_out = jnp.concatenate([x[hs, :] + _rms(mo_, gpost_ref[...]) for mo_, hs in zip(mo, halves)], axis=0)
    xo_ref[...] = x_out.reshape(xo_ref.shape)

    @pl.when(t == pl.num_programs(1) - 1)
    def _():
        for s in range(n_seq):
            for g in range(N_GROUPS):
                sg = state_ref[s, g]
                acc = sg[:, 0:HEAD_DIM]
                for j in range(1, HEADS_PER_GROUP):
                    acc = acc + sg[:, j * HEAD_DIM:(j + 1) * HEAD_DIM]
                wkvo_ref[s, g] = acc


def _ffn_kernel(x_ref, gpre_ref, wgate_ref, wup_ref, wdown_ref, gpost_ref, o_ref):
    half = x_ref.shape[0] // 2
    xs = [x_ref[0:half, :], x_ref[half:2 * half, :]]
    f = [_rms(x, gpre_ref[...]).astype(BF16) for x in xs]
    gate_up = [(jnp.dot(f_, wgate_ref[...], preferred_element_type=F32),
                jnp.dot(f_, wup_ref[...], preferred_element_type=F32)) for f_ in f]
    hid = [(gate * jax.nn.sigmoid(gate) * up).astype(BF16) for gate, up in gate_up]
    f2 = [jnp.dot(hid_, wdown_ref[...], preferred_element_type=F32) for hid_ in hid]
    o_ref[0:half, :] = xs[0] + _rms(f2[0], gpost_ref[...])
    o_ref[half:2 * half, :] = xs[1] + _rms(f2[1], gpost_ref[...])


def _resident(shape):
    nd = len(shape)
    return pl.BlockSpec(shape, lambda *_: (0,) * nd, pipeline_mode=pl.Buffered(1))


def _mix_call(x, shift0, wkv0_bd, params, *, n_seq, seq_rows, gmlp_chunk, emit_vn):
    batch, seq_len, _ = x.shape
    rows = n_seq * seq_rows
    assert rows == MIX_ROWS and batch % n_seq == 0 and seq_len % seq_rows == 0
    assert seq_rows % gmlp_chunk == 0 and seq_rows % WKV_CHUNK == 0
    grid = (batch // n_seq, seq_len // seq_rows)

    ws = jnp.tile(params['w_spatial'][:, :gmlp_chunk, :gmlp_chunk], (1, rows // gmlp_chunk, rows // gmlp_chunk))
    bias = jnp.repeat(params['b_spatial'][:, :gmlp_chunk].T, GMLP_GC, axis=1)
    weights = [params['g_pre_mix'], params['w_in'], params['mu_shift'], params['w0'], params['w_wa'], params['a0'],
               params['w_lora_g'], params['k_k'], params['k_a'], params['r_k'], params['gn_g'], params['gn_b'],
               params['ln_v_g'], params['ln_v_b'], ws, bias, params['w_proj_a'], params['w_proj_b'], params['w_o'],
               params['g_post_mix']]

    x_spec = pl.BlockSpec((n_seq, seq_rows, D_MODEL), lambda b, t: (b, t, 0))
    per_seq = lambda shape: pl.BlockSpec((n_seq,) + shape, lambda b, t: (b,) + (0,) * len(shape),
                                         pipeline_mode=pl.Buffered(1))
    in_specs = [x_spec, per_seq((1, SHIFT_W)), per_seq((N_GROUPS, MXU_TILE, HEAD_DIM))]
    in_specs += [_resident(w.shape) for w in weights]
    out_shape = [jax.ShapeDtypeStruct(x.shape, F32),
                 jax.ShapeDtypeStruct((batch, 1, SHIFT_W), F32),
                 jax.ShapeDtypeStruct((batch, N_GROUPS, MXU_TILE, HEAD_DIM), F32)]
    out_specs = [x_spec, per_seq((1, SHIFT_W)), per_seq((N_GROUPS, MXU_TILE, HEAD_DIM))]
    if emit_vn:
        out_shape.append(jax.ShapeDtypeStruct((batch, seq_len, GMLP_W), F32))
        out_specs.append(x_spec)
    act = lambda dt: pltpu.VMEM((rows, RWKV_W), dt)
    scratch = [pltpu.VMEM((n_seq, 1, SHIFT_W), F32),
               pltpu.VMEM((n_seq, N_GROUPS, MXU_TILE, MXU_TILE), F32),
               act(BF16), act(BF16), act(BF16), act(BF16), act(BF16), act(BF16), act(BF16),
               pltpu.VMEM((rows // WKV_CHUNK * 8, RWKV_W), F32), act(F32)]
    kern = functools.partial(_mix_kernel, n_seq=n_seq, seq_rows=seq_rows, gmlp_chunk=gmlp_chunk, emit_vn=emit_vn)
    return pl.pallas_call(
        kern, grid=grid, in_specs=in_specs, out_specs=out_specs, out_shape=out_shape, scratch_shapes=scratch,
        compiler_params=pltpu.CompilerParams(dimension_semantics=("arbitrary", "arbitrary"),
                                             vmem_limit_bytes=VMEM_LIMIT),
    )(x, shift0, wkv0_bd, *weights)


def _ffn_call(x2d, params):
    n = x2d.shape[0]
    assert n % FFN_ROWS == 0
    weights = [params['g_pre_ffn'], params['w_gate'], params['w_up'], params['w_down'], params['g_post_ffn']]
    row_spec = pl.BlockSpec((FFN_ROWS, D_MODEL), lambda i: (i, 0))
    return pl.pallas_call(
        _ffn_kernel, grid=(n // FFN_ROWS,), in_specs=[row_spec] + [_resident(w.shape) for w in weights],
        out_specs=row_spec, out_shape=jax.ShapeDtypeStruct(x2d.shape, F32),
        compiler_params=pltpu.CompilerParams(dimension_semantics=("arbitrary",), vmem_limit_bytes=VMEM_LIMIT),
    )(x2d, *weights)


def _layer(x, shift0, wkv0, params, *, n_seq, seq_rows, gmlp_chunk, emit_vn):
    wkv0_g = wkv0.astype(F32).reshape(wkv0.shape[0], N_GROUPS, MXU_TILE, HEAD_DIM)
    outs = _mix_call(x, shift0, wkv0_g, params, n_seq=n_seq, seq_rows=seq_rows,
                     gmlp_chunk=gmlp_chunk, emit_vn=emit_vn)
    x_mid, shift, wkv_g = outs[:3]
    y = _ffn_call(x_mid.reshape(-1, D_MODEL), params).reshape(x.shape)
    return y, shift, wkv_g.reshape(wkv0.shape), (outs[3] if emit_vn else None)


def kernel(x_prompt, x_sample, state_shift, state_wkv, g_pre_mix, w_in, mu_shift, w0, w_lora_w, a0, w_lora_a, w_lora_g, k_k, k_a, r_k, gn_g, gn_b, ln_v_g, ln_v_b, w_spatial, b_spatial, w_proj_a, w_proj_b, w_o, g_post_mix, g_pre_ffn, w_gate, w_up, w_down, g_post_ffn):
    depth = w_in.shape[0]
    yp, ys = x_prompt, x_sample
    bp = x_prompt.shape[0]
    shift_p, wkv_p, shift_s, wkv_s, v_s = [], [], [], [], []
    for l in range(depth):
        row = lambda a: a[l].reshape(1, -1).astype(F32)
        zero = jnp.zeros((LORA_W, RWKV_W), BF16)
        w_wa = jnp.concatenate(
            [jnp.concatenate([w_lora_w[l].astype(BF16), zero], axis=1),
             jnp.concatenate([zero, w_lora_a[l].astype(BF16)], axis=1)], axis=0)
        params = {
            'g_pre_mix': row(g_pre_mix), 'w_in': w_in[l].astype(BF16), 'mu_shift': row(mu_shift), 'w0': row(w0),
            'w_wa': w_wa, 'a0': row(a0), 'w_lora_g': w_lora_g[l].astype(BF16), 'k_k': row(k_k), 'k_a': row(k_a),
            'r_k': row(r_k), 'gn_g': row(gn_g), 'gn_b': row(gn_b), 'ln_v_g': row(ln_v_g), 'ln_v_b': row(ln_v_b),
            'w_spatial': w_spatial[l].astype(F32), 'b_spatial': b_spatial[l].astype(F32),
            'w_proj_a': w_proj_a[l].astype(BF16), 'w_proj_b': w_proj_b[l].astype(BF16), 'w_o': w_o[l].astype(BF16),
            'g_post_mix': row(g_post_mix), 'g_pre_ffn': row(g_pre_ffn), 'w_gate': w_gate[l].astype(BF16),
            'w_up': w_up[l].astype(BF16), 'w_down': w_down[l].astype(BF16), 'g_post_ffn': row(g_post_ffn),
        }
        zero_row = jnp.zeros((bp, 1, SHIFT_W), F32)
        zero_s = jnp.zeros((bp, N_HEADS, HEAD_DIM, HEAD_DIM), F32)
        yp, sp, wp, _ = _layer(yp, zero_row, zero_s, params, n_seq=1, seq_rows=MIX_ROWS,
                               gmlp_chunk=GMLP_CHUNK, emit_vn=False)
        dec_seq = x_sample.shape[1]
        ys, ss, wsm, vs = _layer(ys, state_shift[l], state_wkv[l], params, n_seq=MIX_ROWS // dec_seq,
                                 seq_rows=dec_seq, gmlp_chunk=dec_seq, emit_vn=True)
        shift_p.append(sp); wkv_p.append(wp); shift_s.append(ss); wkv_s.append(wsm); v_s.append(vs)
    return (yp, ys, jnp.stack(shift_p), jnp.stack(wkv_p), jnp.stack(shift_s), jnp.stack(wkv_s), jnp.stack(v_s))
```

```python
import functools

import jax
import jax.numpy as jnp
from jax import lax
from jax.experimental import pallas as pl
from jax.experimental.pallas import tpu as pltpu

F32 = jnp.float32
BF16 = jnp.bfloat16

D_MODEL = 1024
HEAD_DIM = 64
N_HEADS = D_MODEL // HEAD_DIM
RWKV_W = N_HEADS * HEAD_DIM
LORA_W = 64
LORA_A = 64
LORA_G = 128
SHIFT_W = 3 * RWKV_W + LORA_W + LORA_A + LORA_G
GMLP_W = D_MODEL
GMLP_GC = 128
GMLP_CHUNK = 128
GMLP_GROUPS = GMLP_W // GMLP_GC
IN_W = SHIFT_W + 2 * GMLP_W + 2 * D_MODEL
RMS_EPS = 1e-6
GN_EPS = 64e-5
LN_EPS = 1e-5
EXP_NEG_HALF = 0.6065306597126334

MXU_TILE = 256
WKV_CHUNK = 64
HEADS_PER_GROUP = MXU_TILE // HEAD_DIM
N_GROUPS = RWKV_W // MXU_TILE
MIX_ROWS = 256
FFN_ROWS = 1024
FFN_PARTS = 4
VMEM_LIMIT = 60 * 1024 * 1024

_NEUMANN_STEPS = 4


def _dot(a, b):
    return jnp.dot(a.astype(BF16), b.astype(BF16), preferred_element_type=F32)


def _dot_nt(a, b):
    return lax.dot_general(a.astype(BF16), b.astype(BF16), (((1,), (1,)), ((), ())),
                           preferred_element_type=F32)


def _block_id(idx, block):
    assert block & (block - 1) == 0
    return lax.shift_right_logical(idx, block.bit_length() - 1)


def _rms(x, g):
    return x * lax.rsqrt(jnp.mean(x * x, axis=-1, keepdims=True) + RMS_EPS) * g


def _split2(x):
    hi = x.astype(BF16)
    lo = (x - hi.astype(F32)).astype(BF16)
    return hi, lo


def _head_sum(x, ones_bd):
    xb = x.astype(BF16)
    cols = [jnp.dot(xb[:, g * MXU_TILE:(g + 1) * MXU_TILE], ones_bd, preferred_element_type=F32)
            for g in range(N_GROUPS)]
    return jnp.concatenate(cols, axis=1)


def _mix_kernel(x_ref, xnext_ref, shift0_ref, wkv0_ref, gpre_ref, win_ref, mu_ref, w0_ref, wwa_ref, a0_ref, wg_ref,
                kk_ref, ka_ref, rk_ref, gng_ref, gnb_ref, lng_ref, lnb_ref, ws_ref, bias_ref,
                wpa_ref, wpb_ref, wo_ref, gpost_ref,
                xo_ref, shifto_ref, wkvo_ref, *rest, n_seq, seq_rows, gmlp_chunk, emit_vn):
    if emit_vn:
        vno_ref, rest = rest[0], rest[1:]
    (carry_ref, state_ref, at_ref, bt_ref, kt_ref, rt_ref, vb_ref, bh_ref, kh_ref, pc_ref, y_ref, h_ref) = rest
    rows = n_seq * seq_rows
    n_chunks = rows // WKV_CHUNK
    chunks_per_seq = seq_rows // WKV_CHUNK
    t = pl.program_id(1)

    same_head = (_block_id(lax.broadcasted_iota(jnp.int32, (MXU_TILE, MXU_TILE), 0), HEAD_DIM)
                 == _block_id(lax.broadcasted_iota(jnp.int32, (MXU_TILE, MXU_TILE), 1), HEAD_DIM))
    bd_f32 = jnp.where(same_head, 1.0, 0.0)
    bd_bf = bd_f32.astype(BF16)

    @pl.when(t == 0)
    def _():
        carry_ref[...] = shift0_ref[...]
        for s in range(n_seq):
            for g in range(N_GROUPS):
                c = wkv0_ref[s, g]
                state_ref[s, g] = jnp.concatenate([c] * HEADS_PER_GROUP, axis=1) * bd_f32

    @pl.when(jnp.logical_and(pl.program_id(0) == 0, t == 0))
    def _():
        h_ref[...] = _rms(x_ref[...].reshape(rows, D_MODEL), gpre_ref[...]).astype(BF16)

    h = h_ref[...]

    ps = jnp.dot(h, win_ref[:, 0:SHIFT_W], preferred_element_type=F32)
    puv = jnp.dot(h, win_ref[:, SHIFT_W:SHIFT_W + 2 * GMLP_W], preferred_element_type=F32)

    row_id = lax.broadcasted_iota(jnp.int32, (rows, 1), 0)
    p_prev = pltpu.roll(ps, 1, 0)
    for s in range(n_seq):
        p_prev = jnp.where(row_id == s * seq_rows, carry_ref[s], p_prev)
        last = ps[(s + 1) * seq_rows - 1:(s + 1) * seq_rows, :]
        carry_ref[s] = last
        shifto_ref[s] = last
    xm = ps + (p_prev - ps) * mu_ref[...]
    r = xm[:, 0:RWKV_W]
    k = xm[:, RWKV_W:2 * RWKV_W]
    v = xm[:, 2 * RWKV_W:3 * RWKV_W]
    o = 3 * RWKV_W
    dwa = xm[:, o:o + LORA_W + LORA_A]
    dg = xm[:, o + LORA_W + LORA_A:SHIFT_W]
    lane = lax.broadcasted_iota(jnp.int32, (1, LORA_W + LORA_A), 1)
    dwa = jnp.where(lane < LORA_W, jnp.tanh(dwa), dwa)
    lwa = _dot(dwa, wwa_ref[...])
    zw = w0_ref[...] + lwa[:, 0:RWKV_W]
    lw = -EXP_NEG_HALF * jax.nn.sigmoid(zw)
    a_rate = jax.nn.sigmoid(a0_ref[...] + lwa[:, RWKV_W:2 * RWKV_W])
    g_gate = _dot(jax.nn.sigmoid(dg), wg_ref[...])

    kk = k * kk_ref[...]
    kk = kk * lax.rsqrt(jnp.maximum(_head_sum(kk * kk, bd_bf), 1e-24))
    k_mod = k * (1.0 + (a_rate - 1.0) * ka_ref[...])
    bonus = _head_sum(r * k_mod * rk_ref[...], bd_bf) * v

    ri = lax.broadcasted_iota(jnp.int32, (rows, rows), 0)
    ci = lax.broadcasted_iota(jnp.int32, (rows, rows), 1)
    same_chunk = _block_id(ri, WKV_CHUNK) == _block_id(ci, WKV_CHUNK)
    tri = jnp.where(same_chunk & (ci <= ri), 1.0, 0.0).astype(BF16)
    lw_hi, lw_lo = _split2(lw)
    cs = (jnp.dot(tri, lw_hi, preferred_element_type=F32)
          + jnp.dot(tri, lw_lo, preferred_element_type=F32))
    ends = []
    for c in range(n_chunks):
        cs_end = cs[(c + 1) * WKV_CHUNK - 1:(c + 1) * WKV_CHUNK, :]
        p_end = jnp.exp(cs_end)
        pc_ref[c * 8:(c + 1) * 8, :] = jnp.broadcast_to(p_end, (8, RWKV_W))
        ends.append(jnp.broadcast_to(p_end, (WKV_CHUNK, RWKV_W)))
    p_inv = jnp.exp(-cs)
    to_end = jnp.concatenate(ends, axis=0) * p_inv
    at_ref[...] = (-kk * jnp.exp(cs - lw)).astype(BF16)
    b_vec = kk * a_rate
    bt_ref[...] = (b_vec * p_inv).astype(BF16)
    kt_ref[...] = (k_mod * p_inv).astype(BF16)
    rt_ref[...] = (r * jnp.exp(cs)).astype(BF16)
    vb_ref[...] = v.astype(BF16)
    bh_ref[...] = (b_vec * to_end).astype(BF16)
    kh_ref[...] = (k_mod * to_end).astype(BF16)

    pu = puv[:, 0:GMLP_W]
    pv = puv[:, GMLP_W:2 * GMLP_W]
    mu_v = jnp.mean(pv, axis=-1, keepdims=True)
    pvc = pv - mu_v
    var_v = jnp.mean(pvc * pvc, axis=-1, keepdims=True)
    vn = pvc * lax.rsqrt(var_v + LN_EPS) * lng_ref[...] + lnb_ref[...]
    if emit_vn:
        vno_ref[...] = vn.reshape(vno_ref.shape)
    vn_bf = vn.astype(BF16)
    sp_mask = (_block_id(ri, gmlp_chunk) == _block_id(ci, gmlp_chunk)) & (ci <= ri)
    col = lambda j: slice(j * MXU_TILE, (j + 1) * MXU_TILE)
    zs = [None] * GMLP_GROUPS
    projb_cols = [None] * (D_MODEL // MXU_TILE)
    pg_cols = [None] * (2 * D_MODEL // MXU_TILE)
    ob_box = []

    def spatial(g):
        ws_g = jnp.where(sp_mask, ws_ref[g], 0.0).astype(BF16)
        zs[g] = jnp.dot(ws_g, vn_bf[:, g * GMLP_GC:(g + 1) * GMLP_GC], preferred_element_type=F32)

    def proj_b_col(j):
        if not ob_box:
            z = jnp.concatenate(zs, axis=1) + jnp.concatenate([bias_ref[...]] * (rows // gmlp_chunk), axis=0)
            ob_box.append((pu * z).astype(BF16))
        projb_cols[j] = jnp.dot(ob_box[0], wpb_ref[:, col(j)], preferred_element_type=F32)

    def gate_col(j):
        off = SHIFT_W + 2 * GMLP_W
        pg_cols[j] = jnp.dot(h, win_ref[:, off + j * MXU_TILE:off + (j + 1) * MXU_TILE],
                             preferred_element_type=F32)

    fillers = ([functools.partial(spatial, g) for g in range(GMLP_GROUPS)]
               + [functools.partial(proj_b_col, j) for j in range(len(projb_cols))]
               + [functools.partial(gate_col, j) for j in range(len(pg_cols))])
    fillers.reverse()

    def fill(n=1):
        for _ in range(n):
            if fillers:
                fillers.pop()()

    per_state_stage = -(-len(fillers) // (2 * chunks_per_seq))

    tq = lax.broadcasted_iota(jnp.int32, (WKV_CHUNK, MXU_TILE), 0)
    sq = lax.broadcasted_iota(jnp.int32, (WKV_CHUNK, MXU_TILE), 1) & (WKV_CHUNK - 1)
    strict = sq < tq
    incl = sq <= tq
    eye_all = jnp.where(sq == tq, 1.0, 0.0)

    def bdiag(m):
        m = m.astype(BF16)
        return jnp.concatenate([m] * HEADS_PER_GROUP, axis=0) * bd_bf

    groups = range(N_GROUPS)
    lanes = [col(g) for g in groups]
    stack = lambda top_, bottom_: jnp.concatenate([top_.astype(BF16), bottom_.astype(BF16)], axis=0)
    top = lambda m: m[:WKV_CHUNK]
    bottom = lambda m: m[WKV_CHUNK:]
    row_sl = [slice(c * WKV_CHUNK, (c + 1) * WKV_CHUNK) for c in range(n_chunks)]
    chains = [(c, g) for c in range(n_chunks) for g in groups]
    idx = {cg: n for n, cg in enumerate(chains)}
    load = lambda ref: [ref[row_sl[c], lanes[g]] for c, g in chains]
    each = lambda fn, *lists: [fn(*vals) for vals in zip(*lists)]
    at, rt, vv = load(at_ref), load(rt_ref), load(vb_ref)
    lhs = each(stack, at, rt)
    s1 = each(lambda l, m: _dot_nt(l, bdiag(m)), lhs, load(bt_ref))
    s2 = each(lambda l, m: _dot_nt(l, bdiag(m)), lhs, load(kt_ref))
    l_ab = [jnp.where(strict, top(s), 0.0) for s in s1]
    a_rb = [jnp.where(incl, bottom(s), 0.0) for s in s1]
    l_ak = [jnp.where(strict, top(s), 0.0) for s in s2]
    a_rk = [jnp.where(incl, bottom(s), 0.0) for s in s2]
    l_pow = [_dot(l, bdiag(l)) for l in l_ab]
    t_inv = [eye_all + l for l in l_ab]
    for _ in range(_NEUMANN_STEPS):
        out = each(lambda t_, p_: _dot(stack(t_, p_), bdiag(p_)), t_inv, l_pow)
        t_inv = each(lambda t_, o_: t_ + top(o_), t_inv, out)
        l_pow = [bottom(o_) for o_ in out]
    t_inv = each(lambda t_, p_: t_ + _dot(t_, bdiag(p_)), t_inv, l_pow)
    out = each(lambda a_, b_, v_: _dot(stack(a_, b_), bdiag(v_)), l_ak, a_rk, vv)
    x_mid = [top(o_) for o_ in out]
    av = [bottom(o_) for o_ in out]
    tg = each(lambda t_, a_: stack(t_, _dot(a_, bdiag(t_))), t_inv, a_rb)
    out = each(lambda tg_, a_: _dot(tg_, bdiag(a_)), tg, at)
    w_mat = [top(o_) for o_ in out]
    q_hat = each(lambda r__, o_: r__.astype(F32) + bottom(o_), rt, out)
    out = each(lambda tg_, x_: _dot(tg_, bdiag(x_)), tg, x_mid)
    u0 = [top(o_) for o_ in out]
    y0 = each(lambda o_, v_: bottom(o_) + v_, out, av)
    qw = each(stack, q_hat, w_mat)
    bk = each(stack, load(bh_ref), load(kh_ref))
    st = {(s, g): state_ref[s, g] for s in range(n_seq) for g in groups}
    for j in range(chunks_per_seq):
        now = [(s, s * chunks_per_seq + j, g) for s in range(n_seq) for g in groups]
        out = {(s, g): _dot_nt(qw[idx[c, g]], st[s, g]) for s, c, g in now}
        fill(per_state_stage)
        for s, c, g in now:
            y_ref[row_sl[c], lanes[g]] = top(out[s, g]) + y0[idx[c, g]]
        upd = {(s, g): _dot(jnp.concatenate([bottom(out[s, g]) + u0[idx[c, g]], vv[idx[c, g]].astype(F32)],
                                            axis=0).T, bk[idx[c, g]]) for s, c, g in now}
        fill(per_state_stage)
        st = {(s, g): st[s, g] * pc_ref[c * 8:c * 8 + 1, lanes[g]] + upd[s, g] * bd_f32 for s, c, g in now}
    for (s, g), val in st.items():
        state_ref[s, g] = val
    fill(len(fillers))

    proj_b = jnp.concatenate(projb_cols, axis=1)
    pg = jnp.concatenate(pg_cols, axis=1)
    halves = [slice(0, rows // 2), slice(rows // 2, rows)]
    inv_hd = 1.0 / HEAD_DIM
    y = [y_ref[hs, :] for hs in halves]
    mean = [_head_sum(y_, bd_bf) * inv_hd for y_ in y]
    yc = [y_ - m_ for y_, m_ in zip(y, mean)]
    var = [_head_sum(yc_ * yc_, bd_bf) * inv_hd for yc_ in yc]
    o_a = [(yc_ * lax.rsqrt(var_ + GN_EPS) * gng_ref[...] + gnb_ref[...] + bonus[hs, :]) * g_gate[hs, :]
           for yc_, var_, hs in zip(yc, var, halves)]
    proj_a = [_dot(o_, wpa_ref[...]) for o_ in o_a]
    m = [jax.nn.sigmoid(pg[hs, 0:D_MODEL]) * pa_ + jax.nn.sigmoid(pg[hs, D_MODEL:2 * D_MODEL]) * proj_b[hs, :]
         for pa_, hs in zip(proj_a, halves)]
    mo = [_dot(m_, wo_ref[...]) for m_ in m]
    x = x_ref[...].reshape(rows, D_MODEL)
    x_out = jnp.concatenate([x[hs, :] + _rms(mo_, gpost_ref[...]) for mo_, hs in zip(mo, halves)], axis=0)
    xo_ref[...] = x_out.reshape(xo_ref.shape)
    h_ref[...] = _rms(xnext_ref[...].reshape(rows, D_MODEL), gpre_ref[...]).astype(BF16)

    @pl.when(t == pl.num_programs(1) - 1)
    def _():
        for s in range(n_seq):
            for g in range(N_GROUPS):
                sg = state_ref[s, g]
                acc = sg[:, 0:HEAD_DIM]
                for j in range(1, HEADS_PER_GROUP):
                    acc = acc + sg[:, j * HEAD_DIM:(j + 1) * HEAD_DIM]
                wkvo_ref[s, g] = acc


def _ffn_kernel(x_ref, gpre_ref, wgate_ref, wup_ref, wdown_ref, gpost_ref, o_ref):
    n = x_ref.shape[0] // FFN_PARTS
    parts = [slice(i * n, (i + 1) * n) for i in range(FFN_PARTS)]
    xs = [x_ref[p, :] for p in parts]
    f = [_rms(x, gpre_ref[...]).astype(BF16) for x in xs]
    gate_up = [(jnp.dot(f_, wgate_ref[...], preferred_element_type=F32),
                jnp.dot(f_, wup_ref[...], preferred_element_type=F32)) for f_ in f]
    hid = [(gate * jax.nn.sigmoid(gate) * up).astype(BF16) for gate, up in gate_up]
    f2 = [jnp.dot(hid_, wdown_ref[...], preferred_element_type=F32) for hid_ in hid]
    for p, x, f2_ in zip(parts, xs, f2):
        o_ref[p, :] = x + _rms(f2_, gpost_ref[...])


def _resident(shape):
    nd = len(shape)
    return pl.BlockSpec(shape, lambda *_: (0,) * nd, pipeline_mode=pl.Buffered(1))


def _mix_call(x, shift0, wkv0_bd, params, *, n_seq, seq_rows, gmlp_chunk, emit_vn):
    batch, seq_len, _ = x.shape
    rows = n_seq * seq_rows
    assert rows == MIX_ROWS and batch % n_seq == 0 and seq_len % seq_rows == 0
    assert seq_rows % gmlp_chunk == 0 and seq_rows % WKV_CHUNK == 0
    grid = (batch // n_seq, seq_len // seq_rows)

    ws = jnp.tile(params['w_spatial'][:, :gmlp_chunk, :gmlp_chunk], (1, rows // gmlp_chunk, rows // gmlp_chunk))
    bias = jnp.repeat(params['b_spatial'][:, :gmlp_chunk].T, GMLP_GC, axis=1)
    weights = [params['g_pre_mix'], params['w_in'], params['mu_shift'], params['w0'], params['w_wa'], params['a0'],
               params['w_lora_g'], params['k_k'], params['k_a'], params['r_k'], params['gn_g'], params['gn_b'],
               params['ln_v_g'], params['ln_v_b'], ws, bias, params['w_proj_a'], params['w_proj_b'], params['w_o'],
               params['g_post_mix']]

    x_spec = pl.BlockSpec((n_seq, seq_rows, D_MODEL), lambda b, t: (b, t, 0))

    def next_tile(b, t):
        flat = jnp.minimum(b * grid[1] + t + 1, grid[0] * grid[1] - 1)
        return flat // grid[1], lax.rem(flat, grid[1]), 0

    xnext_spec = pl.BlockSpec((n_seq, seq_rows, D_MODEL), next_tile)
    per_seq = lambda shape: pl.BlockSpec((n_seq,) + shape, lambda b, t: (b,) + (0,) * len(shape),
                                         pipeline_mode=pl.Buffered(1))
    in_specs = [x_spec, xnext_spec, per_seq((1, SHIFT_W)), per_seq((N_GROUPS, MXU_TILE, HEAD_DIM))]
    in_specs += [_resident(w.shape) for w in weights]
    out_shape = [jax.ShapeDtypeStruct(x.shape, F32),
                 jax.ShapeDtypeStruct((batch, 1, SHIFT_W), F32),
                 jax.ShapeDtypeStruct((batch, N_GROUPS, MXU_TILE, HEAD_DIM), F32)]
    out_specs = [x_spec, per_seq((1, SHIFT_W)), per_seq((N_GROUPS, MXU_TILE, HEAD_DIM))]
    if emit_vn:
        out_shape.append(jax.ShapeDtypeStruct((batch, seq_len, GMLP_W), F32))
        out_specs.append(x_spec)
    act = lambda dt: pltpu.VMEM((rows, RWKV_W), dt)
    scratch = [pltpu.VMEM((n_seq, 1, SHIFT_W), F32),
               pltpu.VMEM((n_seq, N_GROUPS, MXU_TILE, MXU_TILE), F32),
               act(BF16), act(BF16), act(BF16), act(BF16), act(BF16), act(BF16), act(BF16),
               pltpu.VMEM((rows // WKV_CHUNK * 8, RWKV_W), F32), act(F32), pltpu.VMEM((rows, D_MODEL), BF16)]
    kern = functools.partial(_mix_kernel, n_seq=n_seq, seq_rows=seq_rows, gmlp_chunk=gmlp_chunk, emit_vn=emit_vn)
    return pl.pallas_call(
        kern, grid=grid, in_specs=in_specs, out_specs=out_specs, out_shape=out_shape, scratch_shapes=scratch,
        compiler_params=pltpu.CompilerParams(dimension_semantics=("arbitrary", "arbitrary"),
                                             vmem_limit_bytes=VMEM_LIMIT),
    )(x, x, shift0, wkv0_bd, *weights)


def _ffn_call(x2d, params):
    n = x2d.shape[0]
    assert n % FFN_ROWS == 0
    weights = [params['g_pre_ffn'], params['w_gate'], params['w_up'], params['w_down'], params['g_post_ffn']]
    row_spec = pl.BlockSpec((FFN_ROWS, D_MODEL), lambda i: (i, 0))
    return pl.pallas_call(
        _ffn_kernel, grid=(n // FFN_ROWS,), in_specs=[row_spec] + [_resident(w.shape) for w in weights],
        out_specs=row_spec, out_shape=jax.ShapeDtypeStruct(x2d.shape, F32),
        compiler_params=pltpu.CompilerParams(dimension_semantics=("arbitrary",), vmem_limit_bytes=VMEM_LIMIT),
    )(x2d, *weights)


def _layer(x, shift0, wkv0, params, *, n_seq, seq_rows, gmlp_chunk, emit_vn):
    wkv0_g = wkv0.astype(F32).reshape(wkv0.shape[0], N_GROUPS, MXU_TILE, HEAD_DIM)
    outs = _mix_call(x, shift0, wkv0_g, params, n_seq=n_seq, seq_rows=seq_rows,
                     gmlp_chunk=gmlp_chunk, emit_vn=emit_vn)
    x_mid, shift, wkv_g = outs[:3]
    y = _ffn_call(x_mid.reshape(-1, D_MODEL), params).reshape(x.shape)
    return y, shift, wkv_g.reshape(wkv0.shape), (outs[3] if emit_vn else None)


def kernel(x_prompt, x_sample, state_shift, state_wkv, g_pre_mix, w_in, mu_shift, w0, w_lora_w, a0, w_lora_a, w_lora_g, k_k, k_a, r_k, gn_g, gn_b, ln_v_g, ln_v_b, w_spatial, b_spatial, w_proj_a, w_proj_b, w_o, g_post_mix, g_pre_ffn, w_gate, w_up, w_down, g_post_ffn):
    depth = w_in.shape[0]
    yp, ys = x_prompt, x_sample
    bp = x_prompt.shape[0]
    shift_p, wkv_p, shift_s, wkv_s, v_s = [], [], [], [], []
    for l in range(depth):
        row = lambda a: a[l].reshape(1, -1).astype(F32)
        zero = jnp.zeros((LORA_W, RWKV_W), BF16)
        w_wa = jnp.concatenate(
            [jnp.concatenate([w_lora_w[l].astype(BF16), zero], axis=1),
             jnp.concatenate([zero, w_lora_a[l].astype(BF16)], axis=1)], axis=0)
        params = {
            'g_pre_mix': row(g_pre_mix), 'w_in': w_in[l].astype(BF16), 'mu_shift': row(mu_shift), 'w0': row(w0),
            'w_wa': w_wa, 'a0': row(a0), 'w_lora_g': w_lora_g[l].astype(BF16), 'k_k': row(k_k), 'k_a': row(k_a),
            'r_k': row(r_k), 'gn_g': row(gn_g), 'gn_b': row(gn_b), 'ln_v_g': row(ln_v_g), 'ln_v_b': row(ln_v_b),
            'w_spatial': w_spatial[l].astype(F32), 'b_spatial': b_spatial[l].astype(F32),
            'w_proj_a': w_proj_a[l].astype(BF16), 'w_proj_b': w_proj_b[l].astype(BF16), 'w_o': w_o[l].astype(BF16),
            'g_post_mix': row(g_post_mix), 'g_pre_ffn': row(g_pre_ffn), 'w_gate': w_gate[l].astype(BF16),
            'w_up': w_up[l].astype(BF16), 'w_down': w_down[l].astype(BF16), 'g_post_ffn': row(g_post_ffn),
        }
        zero_row = jnp.zeros((bp, 1, SHIFT_W), F32)
        zero_s = jnp.zeros((bp, N_HEADS, HEAD_DIM, HEAD_DIM), F32)
        yp, sp, wp, _ = _layer(yp, zero_row, zero_s, params, n_seq=1, seq_rows=MIX_ROWS,
                               gmlp_chunk=GMLP_CHUNK, emit_vn=False)
        dec_seq = x_sample.shape[1]
        ys, ss, wsm, vs = _layer(ys, state_shift[l], state_wkv[l], params, n_seq=MIX_ROWS // dec_seq,
                                 seq_rows=dec_seq, gmlp_chunk=dec_seq, emit_vn=True)
        shift_p.append(sp); wkv_p.append(wp); shift_s.append(ss); wkv_s.append(wsm); v_s.append(vs)
    return (yp, ys, jnp.stack(shift_p), jnp.stack(wkv_p), jnp.stack(shift_s), jnp.stack(wkv_s), jnp.stack(v_s))
```

```python
import functools

import jax
import jax.numpy as jnp
from jax import lax
from jax.experimental import pallas as pl
from jax.experimental.pallas import tpu as pltpu

F32 = jnp.float32
BF16 = jnp.bfloat16

D_MODEL = 1024
HEAD_DIM = 64
N_HEADS = D_MODEL // HEAD_DIM
RWKV_W = N_HEADS * HEAD_DIM
LORA_W = 64
LORA_A = 64
LORA_G = 128
SHIFT_W = 3 * RWKV_W + LORA_W + LORA_A + LORA_G
GMLP_W = D_MODEL
GMLP_GC = 128
GMLP_CHUNK = 128
GMLP_GROUPS = GMLP_W // GMLP_GC
IN_W = SHIFT_W + 2 * GMLP_W + 2 * D_MODEL
RMS_EPS = 1e-6
GN_EPS = 64e-5
LN_EPS = 1e-5
EXP_NEG_HALF = 0.6065306597126334
KK_NORM_FLOOR = 1e-12

SUBLANES = 8

MXU_TILE = 256
WKV_CHUNK = 64
HEADS_PER_GROUP = MXU_TILE // HEAD_DIM
N_GROUPS = RWKV_W // MXU_TILE
MIX_ROWS = 256
FFN_ROWS = 1024
FFN_PARTS = 4
VMEM_LIMIT = 60 * 1024 * 1024

_NEUMANN_STEPS = 4


def _dot(a, b):
    return jnp.dot(a.astype(BF16), b.astype(BF16), preferred_element_type=F32)


def _dot_nt(a, b):
    return lax.dot_general(a.astype(BF16), b.astype(BF16), (((1,), (1,)), ((), ())),
                           preferred_element_type=F32)


def _block_id(idx, block):
    assert block & (block - 1) == 0
    return lax.shift_right_logical(idx, block.bit_length() - 1)


def _rms(x, g):
    return x * lax.rsqrt(jnp.mean(x * x, axis=-1, keepdims=True) + RMS_EPS) * g


def _split2(x):
    hi = x.astype(BF16)
    lo = (x - hi.astype(F32)).astype(BF16)
    return hi, lo


def _head_sum(x, ones_bd):
    xb = x.astype(BF16)
    cols = [jnp.dot(xb[:, g * MXU_TILE:(g + 1) * MXU_TILE], ones_bd, preferred_element_type=F32)
            for g in range(N_GROUPS)]
    return jnp.concatenate(cols, axis=1)


def _mix_kernel(x_ref, xnext_ref, shift0_ref, wkv0_ref, gpre_ref, win_ref, mu_ref, w0_ref, wwa_ref, a0_ref, wg_ref,
                kk_ref, ka_ref, rk_ref, gng_ref, gnb_ref, lng_ref, lnb_ref, ws_ref, bias_ref,
                wpa_ref, wpb_ref, wo_ref, gpost_ref,
                xo_ref, shifto_ref, wkvo_ref, *rest, n_seq, seq_rows, gmlp_chunk, emit_vn):
    if emit_vn:
        vno_ref, rest = rest[0], rest[1:]
    (carry_ref, state_ref, at_ref, bt_ref, kt_ref, rt_ref, vb_ref, bh_ref, kh_ref, pc_ref, y_ref, h_ref) = rest
    rows = n_seq * seq_rows
    n_chunks = rows // WKV_CHUNK
    chunks_per_seq = seq_rows // WKV_CHUNK
    t = pl.program_id(1)

    same_head = (_block_id(lax.broadcasted_iota(jnp.int32, (MXU_TILE, MXU_TILE), 0), HEAD_DIM)
                 == _block_id(lax.broadcasted_iota(jnp.int32, (MXU_TILE, MXU_TILE), 1), HEAD_DIM))
    bd_f32 = jnp.where(same_head, 1.0, 0.0)
    bd_bf = bd_f32.astype(BF16)

    @pl.when(t == 0)
    def _():
        carry_ref[...] = shift0_ref[...]
        for s in range(n_seq):
            for g in range(N_GROUPS):
                c = wkv0_ref[s, g]
                state_ref[s, g] = jnp.concatenate([c] * HEADS_PER_GROUP, axis=1) * bd_f32

    @pl.when(jnp.logical_and(pl.program_id(0) == 0, t == 0))
    def _():
        h_ref[...] = _rms(x_ref[...].reshape(rows, D_MODEL), gpre_ref[...]).astype(BF16)

    h = h_ref[...]

    ps = jnp.dot(h, win_ref[:, 0:SHIFT_W], preferred_element_type=F32)
    puv = jnp.dot(h, win_ref[:, SHIFT_W:SHIFT_W + 2 * GMLP_W], preferred_element_type=F32)

    row_id = lax.broadcasted_iota(jnp.int32, (rows, 1), 0)
    p_prev = pltpu.roll(ps, 1, 0)
    for s in range(n_seq):
        p_prev = jnp.where(row_id == s * seq_rows, carry_ref[s], p_prev)
        last = ps[(s + 1) * seq_rows - 1:(s + 1) * seq_rows, :]
        carry_ref[s] = last
        shifto_ref[s] = last
    xm = ps + (p_prev - ps) * mu_ref[...]
    r = xm[:, 0:RWKV_W]
    k = xm[:, RWKV_W:2 * RWKV_W]
    v = xm[:, 2 * RWKV_W:3 * RWKV_W]
    o = 3 * RWKV_W
    dwa = xm[:, o:o + LORA_W + LORA_A]
    dg = xm[:, o + LORA_W + LORA_A:SHIFT_W]
    lane = lax.broadcasted_iota(jnp.int32, (1, LORA_W + LORA_A), 1)
    dwa = jnp.where(lane < LORA_W, jnp.tanh(dwa), dwa)
    lwa = _dot(dwa, wwa_ref[...])
    zw = w0_ref[...] + lwa[:, 0:RWKV_W]
    lw = -EXP_NEG_HALF * jax.nn.sigmoid(zw)
    a_rate = jax.nn.sigmoid(a0_ref[...] + lwa[:, RWKV_W:2 * RWKV_W])
    g_gate = _dot(jax.nn.sigmoid(dg), wg_ref[...])

    kk = k * kk_ref[...]
    kk = kk * lax.rsqrt(jnp.maximum(_head_sum(kk * kk, bd_bf), KK_NORM_FLOOR * KK_NORM_FLOOR))
    k_mod = k * (1.0 + (a_rate - 1.0) * ka_ref[...])
    bonus = _head_sum(r * k_mod * rk_ref[...], bd_bf) * v

    ri = lax.broadcasted_iota(jnp.int32, (rows, rows), 0)
    ci = lax.broadcasted_iota(jnp.int32, (rows, rows), 1)
    same_chunk = _block_id(ri, WKV_CHUNK) == _block_id(ci, WKV_CHUNK)
    tri = jnp.where(same_chunk & (ci <= ri), 1.0, 0.0).astype(BF16)
    lw_hi, lw_lo = _split2(lw)
    cs = (jnp.dot(tri, lw_hi, preferred_element_type=F32)
          + jnp.dot(tri, lw_lo, preferred_element_type=F32))
    ends = []
    for c in range(n_chunks):
        cs_end = cs[(c + 1) * WKV_CHUNK - 1:(c + 1) * WKV_CHUNK, :]
        p_end = jnp.exp(cs_end)
        pc_ref[c * SUBLANES:(c + 1) * SUBLANES, :] = jnp.broadcast_to(p_end, (SUBLANES, RWKV_W))
        ends.append(jnp.broadcast_to(p_end, (WKV_CHUNK, RWKV_W)))
    p_inv = jnp.exp(-cs)
    to_end = jnp.concatenate(ends, axis=0) * p_inv
    at_ref[...] = (-kk * jnp.exp(cs - lw)).astype(BF16)
    b_vec = kk * a_rate
    bt_ref[...] = (b_vec * p_inv).astype(BF16)
    kt_ref[...] = (k_mod * p_inv).astype(BF16)
    rt_ref[...] = (r * jnp.exp(cs)).astype(BF16)
    vb_ref[...] = v.astype(BF16)
    bh_ref[...] = (b_vec * to_end).astype(BF16)
    kh_ref[...] = (k_mod * to_end).astype(BF16)

    pu = puv[:, 0:GMLP_W]
    pv = puv[:, GMLP_W:2 * GMLP_W]
    mu_v = jnp.mean(pv, axis=-1, keepdims=True)
    pvc = pv - mu_v
    var_v = jnp.mean(pvc * pvc, axis=-1, keepdims=True)
    vn = pvc * lax.rsqrt(var_v + LN_EPS) * lng_ref[...] + lnb_ref[...]
    if emit_vn:
        vno_ref[...] = vn.reshape(vno_ref.shape)
    vn_bf = vn.astype(BF16)
    sp_mask = (_block_id(ri, gmlp_chunk) == _block_id(ci, gmlp_chunk)) & (ci <= ri)
    col = lambda j: slice(j * MXU_TILE, (j + 1) * MXU_TILE)
    zs = [None] * GMLP_GROUPS
    projb_cols = [None] * (D_MODEL // MXU_TILE)
    pg_cols = [None] * (2 * D_MODEL // MXU_TILE)
    ob_box = []

    def spatial(g):
        ws_g = jnp.where(sp_mask, ws_ref[g], 0.0).astype(BF16)
        zs[g] = jnp.dot(ws_g, vn_bf[:, g * GMLP_GC:(g + 1) * GMLP_GC], preferred_element_type=F32)

    def proj_b_col(j):
        if not ob_box:
            z = jnp.concatenate(zs, axis=1) + jnp.concatenate([bias_ref[...]] * (rows // gmlp_chunk), axis=0)
            ob_box.append((pu * z).astype(BF16))
        projb_cols[j] = jnp.dot(ob_box[0], wpb_ref[:, col(j)], preferred_element_type=F32)

    def gate_col(j):
        off = SHIFT_W + 2 * GMLP_W
        pg_cols[j] = jnp.dot(h, win_ref[:, off + j * MXU_TILE:off + (j + 1) * MXU_TILE],
                             preferred_element_type=F32)

    fillers = ([functools.partial(spatial, g) for g in range(GMLP_GROUPS)]
               + [functools.partial(proj_b_col, j) for j in range(len(projb_cols))]
               + [functools.partial(gate_col, j) for j in range(len(pg_cols))])
    fillers.reverse()

    def fill(n=1):
        for _ in range(n):
            if fillers:
                fillers.pop()()

    per_state_stage = -(-len(fillers) // (2 * chunks_per_seq))

    tq = lax.broadcasted_iota(jnp.int32, (WKV_CHUNK, MXU_TILE), 0)
    sq = lax.broadcasted_iota(jnp.int32, (WKV_CHUNK, MXU_TILE), 1) & (WKV_CHUNK - 1)
    strict = sq < tq
    incl = sq <= tq
    eye_all = jnp.where(sq == tq, 1.0, 0.0)

    def bdiag(m):
        m = m.astype(BF16)
        return jnp.concatenate([m] * HEADS_PER_GROUP, axis=0) * bd_bf

    groups = range(N_GROUPS)
    lanes = [col(g) for g in groups]
    stack = lambda top_, bottom_: jnp.concatenate([top_.astype(BF16), bottom_.astype(BF16)], axis=0)
    top = lambda m: m[:WKV_CHUNK]
    bottom = lambda m: m[WKV_CHUNK:]
    row_sl = [slice(c * WKV_CHUNK, (c + 1) * WKV_CHUNK) for c in range(n_chunks)]
    chains = [(c, g) for c in range(n_chunks) for g in groups]
    idx = {cg: n for n, cg in enumerate(chains)}
    load = lambda ref: [ref[row_sl[c], lanes[g]] for c, g in chains]
    each = lambda fn, *lists: [fn(*vals) for vals in zip(*lists)]
    at, rt, vv = load(at_ref), load(rt_ref), load(vb_ref)
    lhs = each(stack, at, rt)
    s1 = each(lambda l, m: _dot_nt(l, bdiag(m)), lhs, load(bt_ref))
    s2 = each(lambda l, m: _dot_nt(l, bdiag(m)), lhs, load(kt_ref))
    l_ab = [jnp.where(strict, top(s), 0.0) for s in s1]
    a_rb = [jnp.where(incl, bottom(s), 0.0) for s in s1]
    l_ak = [jnp.where(strict, top(s), 0.0) for s in s2]
    a_rk = [jnp.where(incl, bottom(s), 0.0) for s in s2]
    l_pow = [_dot(l, bdiag(l)) for l in l_ab]
    t_inv = [eye_all + l for l in l_ab]
    for _ in range(_NEUMANN_STEPS):
        out = each(lambda t_, p_: _dot(stack(t_, p_), bdiag(p_)), t_inv, l_pow)
        t_inv = each(lambda t_, o_: t_ + top(o_), t_inv, out)
        l_pow = [bottom(o_) for o_ in out]
    t_inv = each(lambda t_, p_: t_ + _dot(t_, bdiag(p_)), t_inv, l_pow)
    out = each(lambda a_, b_, v_: _dot(stack(a_, b_), bdiag(v_)), l_ak, a_rk, vv)
    x_mid = [top(o_) for o_ in out]
    av = [bottom(o_) for o_ in out]
    tg = each(lambda t_, a_: stack(t_, _dot(a_, bdiag(t_))), t_inv, a_rb)
    out = each(lambda tg_, a_: _dot(tg_, bdiag(a_)), tg, at)
    w_mat = [top(o_) for o_ in out]
    q_hat = each(lambda r__, o_: r__.astype(F32) + bottom(o_), rt, out)
    out = each(lambda tg_, x_: _dot(tg_, bdiag(x_)), tg, x_mid)
    u0 = [top(o_) for o_ in out]
    y0 = each(lambda o_, v_: bottom(o_) + v_, out, av)
    qw = each(stack, q_hat, w_mat)
    bk = each(stack, load(bh_ref), load(kh_ref))
    st = {(s, g): state_ref[s, g] for s in range(n_seq) for g in groups}
    for j in range(chunks_per_seq):
        now = [(s, s * chunks_per_seq + j, g) for s in range(n_seq) for g in groups]
        out = {(s, g): _dot_nt(qw[idx[c, g]], st[s, g]) for s, c, g in now}
        fill(per_state_stage)
        for s, c, g in now:
            y_ref[row_sl[c], lanes[g]] = top(out[s, g]) + y0[idx[c, g]]
        upd = {(s, g): _dot(jnp.concatenate([bottom(out[s, g]) + u0[idx[c, g]], vv[idx[c, g]].astype(F32)],
                                            axis=0).T, bk[idx[c, g]]) for s, c, g in now}
        fill(per_state_stage)
        st = {(s, g): st[s, g] * pc_ref[c * SUBLANES:c * SUBLANES + 1, lanes[g]] + upd[s, g] * bd_f32
              for s, c, g in now}
    for (s, g), val in st.items():
        state_ref[s, g] = val
    fill(len(fillers))

    proj_b = jnp.concatenate(projb_cols, axis=1)
    pg = jnp.concatenate(pg_cols, axis=1)
    halves = [slice(0, rows // 2), slice(rows // 2, rows)]
    inv_hd = 1.0 / HEAD_DIM
    y = [y_ref[hs, :] for hs in halves]
    mean = [_head_sum(y_, bd_bf) * inv_hd for y_ in y]
    yc = [y_ - m_ for y_, m_ in zip(y, mean)]
    var = [_head_sum(yc_ * yc_, bd_bf) * inv_hd for yc_ in yc]
    o_a = [(yc_ * lax.rsqrt(var_ + GN_EPS) * gng_ref[...] + gnb_ref[...] + bonus[hs, :]) * g_gate[hs, :]
           for yc_, var_, hs in zip(yc, var, halves)]
    proj_a = [_dot(o_, wpa_ref[...]) for o_ in o_a]
    m = [jax.nn.sigmoid(pg[hs, 0:D_MODEL]) * pa_ + jax.nn.sigmoid(pg[hs, D_MODEL:2 * D_MODEL]) * proj_b[hs, :]
         for pa_, hs in zip(proj_a, halves)]
    mo = [_dot(m_, wo_ref[...]) for m_ in m]
    x = x_ref[...].reshape(rows, D_MODEL)
    x_out = jnp.concatenate([x[hs, :] + _rms(mo_, gpost_ref[...]) for mo_, hs in zip(mo, halves)], axis=0)
    xo_ref[...] = x_out.reshape(xo_ref.shape)
    h_ref[...] = _rms(xnext_ref[...].reshape(rows, D_MODEL), gpre_ref[...]).astype(BF16)

    @pl.when(t == pl.num_programs(1) - 1)
    def _():
        for s in range(n_seq):
            for g in range(N_GROUPS):
                sg = state_ref[s, g]
                acc = sg[:, 0:HEAD_DIM]
                for j in range(1, HEADS_PER_GROUP):
                    acc = acc + sg[:, j * HEAD_DIM:(j + 1) * HEAD_DIM]
                wkvo_ref[s, g] = acc


def _ffn_kernel(x_ref, gpre_ref, wgate_ref, wup_ref, wdown_ref, gpost_ref, o_ref):
    n = x_ref.shape[0] // FFN_PARTS
    parts = [slice(i * n, (i + 1) * n) for i in range(FFN_PARTS)]
    xs = [x_ref[p, :] for p in parts]
    f = [_rms(x, gpre_ref[...]).astype(BF16) for x in xs]
    gate_up = [(jnp.dot(f_, wgate_ref[...], preferred_element_type=F32),
                jnp.dot(f_, wup_ref[...], preferred_element_type=F32)) for f_ in f]
    hid = [(gate * jax.nn.sigmoid(gate) * up).astype(BF16) for gate, up in gate_up]
    f2 = [jnp.dot(hid_, wdown_ref[...], preferred_element_type=F32) for hid_ in hid]
    for p, x, f2_ in zip(parts, xs, f2):
        o_ref[p, :] = x + _rms(f2_, gpost_ref[...])


def _resident(shape):
    nd = len(shape)
    return pl.BlockSpec(shape, lambda *_: (0,) * nd, pipeline_mode=pl.Buffered(1))


def _mix_call(x, shift0, wkv0_bd, params, *, n_seq, seq_rows, gmlp_chunk, emit_vn):
    batch, seq_len, _ = x.shape
    rows = n_seq * seq_rows
    assert rows == MIX_ROWS and batch % n_seq == 0 and seq_len % seq_rows == 0
    assert seq_rows % gmlp_chunk == 0 and seq_rows % WKV_CHUNK == 0
    grid = (batch // n_seq, seq_len // seq_rows)

    ws = jnp.tile(params['w_spatial'][:, :gmlp_chunk, :gmlp_chunk], (1, rows // gmlp_chunk, rows // gmlp_chunk))
    bias = jnp.repeat(params['b_spatial'][:, :gmlp_chunk].T, GMLP_GC, axis=1)
    weights = [params['g_pre_mix'], params['w_in'], params['mu_shift'], params['w0'], params['w_wa'], params['a0'],
               params['w_lora_g'], params['k_k'], params['k_a'], params['r_k'], params['gn_g'], params['gn_b'],
               params['ln_v_g'], params['ln_v_b'], ws, bias, params['w_proj_a'], params['w_proj_b'], params['w_o'],
               params['g_post_mix']]

    x_spec = pl.BlockSpec((n_seq, seq_rows, D_MODEL), lambda b, t: (b, t, 0))

    def next_tile(b, t):
        flat = jnp.minimum(b * grid[1] + t + 1, grid[0] * grid[1] - 1)
        return flat // grid[1], lax.rem(flat, grid[1]), 0

    xnext_spec = pl.BlockSpec((n_seq, seq_rows, D_MODEL), next_tile)
    per_seq = lambda shape: pl.BlockSpec((n_seq,) + shape, lambda b, t: (b,) + (0,) * len(shape),
                                         pipeline_mode=pl.Buffered(1))
    in_specs = [x_spec, xnext_spec, per_seq((1, SHIFT_W)), per_seq((N_GROUPS, MXU_TILE, HEAD_DIM))]
    in_specs += [_resident(w.shape) for w in weights]
    out_shape = [jax.ShapeDtypeStruct(x.shape, F32),
                 jax.ShapeDtypeStruct((batch, 1, SHIFT_W), F32),
                 jax.ShapeDtypeStruct((batch, N_GROUPS, MXU_TILE, HEAD_DIM), F32)]
    out_specs = [x_spec, per_seq((1, SHIFT_W)), per_seq((N_GROUPS, MXU_TILE, HEAD_DIM))]
    if emit_vn:
        out_shape.append(jax.ShapeDtypeStruct((batch, seq_len, GMLP_W), F32))
        out_specs.append(x_spec)
    act = lambda dt: pltpu.VMEM((rows, RWKV_W), dt)
    scratch = [pltpu.VMEM((n_seq, 1, SHIFT_W), F32),
               pltpu.VMEM((n_seq, N_GROUPS, MXU_TILE, MXU_TILE), F32),
               act(BF16), act(BF16), act(BF16), act(BF16), act(BF16), act(BF16), act(BF16),
               pltpu.VMEM((rows // WKV_CHUNK * SUBLANES, RWKV_W), F32), act(F32),
               pltpu.VMEM((rows, D_MODEL), BF16)]
    kern = functools.partial(_mix_kernel, n_seq=n_seq, seq_rows=seq_rows, gmlp_chunk=gmlp_chunk, emit_vn=emit_vn)
    return pl.pallas_call(
        kern, grid=grid, in_specs=in_specs, out_specs=out_specs, out_shape=out_shape, scratch_shapes=scratch,
        compiler_params=pltpu.CompilerParams(dimension_semantics=("arbitrary", "arbitrary"),
                                             vmem_limit_bytes=VMEM_LIMIT),
    )(x, x, shift0, wkv0_bd, *weights)


def _ffn_call(x2d, params):
    n = x2d.shape[0]
    assert n % FFN_ROWS == 0
    weights = [params['g_pre_ffn'], params['w_gate'], params['w_up'], params['w_down'], params['g_post_ffn']]
    row_spec = pl.BlockSpec((FFN_ROWS, D_MODEL), lambda i: (i, 0))
    return pl.pallas_call(
        _ffn_kernel, grid=(n // FFN_ROWS,), in_specs=[row_spec] + [_resident(w.shape) for w in weights],
        out_specs=row_spec, out_shape=jax.ShapeDtypeStruct(x2d.shape, F32),
        compiler_params=pltpu.CompilerParams(dimension_semantics=("arbitrary",), vmem_limit_bytes=VMEM_LIMIT),
    )(x2d, *weights)


def _layer(x, shift0, wkv0, params, *, n_seq, seq_rows, gmlp_chunk, emit_vn):
    wkv0_g = wkv0.astype(F32).reshape(wkv0.shape[0], N_GROUPS, MXU_TILE, HEAD_DIM)
    outs = _mix_call(x, shift0, wkv0_g, params, n_seq=n_seq, seq_rows=seq_rows,
                     gmlp_chunk=gmlp_chunk, emit_vn=emit_vn)
    x_mid, shift, wkv_g = outs[:3]
    y = _ffn_call(x_mid.reshape(-1, D_MODEL), params).reshape(x.shape)
    return y, shift, wkv_g.reshape(wkv0.shape), (outs[3] if emit_vn else None)


def kernel(x_prompt, x_sample, state_shift, state_wkv, g_pre_mix, w_in, mu_shift, w0, w_lora_w, a0, w_lora_a, w_lora_g, k_k, k_a, r_k, gn_g, gn_b, ln_v_g, ln_v_b, w_spatial, b_spatial, w_proj_a, w_proj_b, w_o, g_post_mix, g_pre_ffn, w_gate, w_up, w_down, g_post_ffn):
    depth = w_in.shape[0]
    yp, ys = x_prompt, x_sample
    bp = x_prompt.shape[0]
    shift_p, wkv_p, shift_s, wkv_s, v_s = [], [], [], [], []
    for l in range(depth):
        row = lambda a: a[l].reshape(1, -1).astype(F32)
        zero = jnp.zeros((LORA_W, RWKV_W), BF16)
        w_wa = jnp.concatenate(
            [jnp.concatenate([w_lora_w[l].astype(BF16), zero], axis=1),
             jnp.concatenate([zero, w_lora_a[l].astype(BF16)], axis=1)], axis=0)
        params = {
            'g_pre_mix': row(g_pre_mix), 'w_in': w_in[l].astype(BF16), 'mu_shift': row(mu_shift), 'w0': row(w0),
            'w_wa': w_wa, 'a0': row(a0), 'w_lora_g': w_lora_g[l].astype(BF16), 'k_k': row(k_k), 'k_a': row(k_a),
            'r_k': row(r_k), 'gn_g': row(gn_g), 'gn_b': row(gn_b), 'ln_v_g': row(ln_v_g), 'ln_v_b': row(ln_v_b),
            'w_spatial': w_spatial[l].astype(F32), 'b_spatial': b_spatial[l].astype(F32),
            'w_proj_a': w_proj_a[l].astype(BF16), 'w_proj_b': w_proj_b[l].astype(BF16), 'w_o': w_o[l].astype(BF16),
            'g_post_mix': row(g_post_mix), 'g_pre_ffn': row(g_pre_ffn), 'w_gate': w_gate[l].astype(BF16),
            'w_up': w_up[l].astype(BF16), 'w_down': w_down[l].astype(BF16), 'g_post_ffn': row(g_post_ffn),
        }
        zero_row = jnp.zeros((bp, 1, SHIFT_W), F32)
        zero_s = jnp.zeros((bp, N_HEADS, HEAD_DIM, HEAD_DIM), F32)
        yp, sp, wp, _ = _layer(yp, zero_row, zero_s, params, n_seq=1, seq_rows=MIX_ROWS,
                               gmlp_chunk=GMLP_CHUNK, emit_vn=False)
        dec_seq = x_sample.shape[1]
        ys, ss, wsm, vs = _layer(ys, state_shift[l], state_wkv[l], params, n_seq=MIX_ROWS // dec_seq,
                                 seq_rows=dec_seq, gmlp_chunk=dec_seq, emit_vn=True)
        shift_p.append(sp); wkv_p.append(wp); shift_s.append(ss); wkv_s.append(wsm); v_s.append(vs)
    return (yp, ys, jnp.stack(shift_p), jnp.stack(wkv_p), jnp.stack(shift_s), jnp.stack(wkv_s), jnp.stack(v_s))
```

```python
import functools

import jax
import jax.numpy as jnp
from jax import lax
from jax.experimental import pallas as pl
from jax.experimental.pallas import tpu as pltpu

F32 = jnp.float32
BF16 = jnp.bfloat16

D_MODEL = 1024
HEAD_DIM = 64
N_HEADS = D_MODEL // HEAD_DIM
RWKV_W = N_HEADS * HEAD_DIM
LORA_W = 64
LORA_A = 64
LORA_G = 128
SHIFT_W = 3 * RWKV_W + LORA_W + LORA_A + LORA_G
GMLP_W = D_MODEL
GMLP_GC = 128
GMLP_CHUNK = 128
GMLP_GROUPS = GMLP_W // GMLP_GC
IN_W = SHIFT_W + 2 * GMLP_W + 2 * D_MODEL
RMS_EPS = 1e-6
GN_EPS = 64e-5
LN_EPS = 1e-5
EXP_NEG_HALF = 0.6065306597126334

MXU_TILE = 256
WKV_CHUNK = 64
HEADS_PER_GROUP = MXU_TILE // HEAD_DIM
N_GROUPS = RWKV_W // MXU_TILE
MIX_ROWS = 256
FFN_ROWS = 1024
FFN_PARTS = 4
VMEM_LIMIT = 60 * 1024 * 1024

_NEUMANN_STEPS = 4


def _dot(a, b):
    return jnp.dot(a.astype(BF16), b.astype(BF16), preferred_element_type=F32)


def _dot_nt(a, b):
    return lax.dot_general(a.astype(BF16), b.astype(BF16), (((1,), (1,)), ((), ())),
                           preferred_element_type=F32)


def _block_id(idx, block):
    assert block & (block - 1) == 0
    return lax.shift_right_logical(idx, block.bit_length() - 1)


def _rms(x, g):
    return x * lax.rsqrt(jnp.mean(x * x, axis=-1, keepdims=True) + RMS_EPS) * g


def _split2(x):
    hi = x.astype(BF16)
    lo = (x - hi.astype(F32)).astype(BF16)
    return hi, lo


def _head_sum(x, ones_bd):
    xb = x.astype(BF16)
    cols = [jnp.dot(xb[:, g * MXU_TILE:(g + 1) * MXU_TILE], ones_bd, preferred_element_type=F32)
            for g in range(N_GROUPS)]
    return jnp.concatenate(cols, axis=1)


def _mix_kernel(x_ref, xnext_ref, shift0_ref, wkv0_ref, gpre_ref, win_ref, mu_ref, w0_ref, wwa_ref, a0_ref, wg_ref,
                kk_ref, ka_ref, rk_ref, gng_ref, gnb_ref, lng_ref, lnb_ref, ws_ref, bias_ref,
                wpa_ref, wpb_ref, wo_ref, gpost_ref,
                xo_ref, shifto_ref, wkvo_ref, *rest, n_seq, seq_rows, gmlp_chunk, emit_vn):
    if emit_vn:
        vno_ref, rest = rest[0], rest[1:]
    (carry_ref, state_ref, at_ref, bt_ref, kt_ref, rt_ref, vb_ref, bh_ref, kh_ref, pc_ref, y_ref, h_ref) = rest
    rows = n_seq * seq_rows
    n_chunks = rows // WKV_CHUNK
    chunks_per_seq = seq_rows // WKV_CHUNK
    t = pl.program_id(1)

    same_head = (_block_id(lax.broadcasted_iota(jnp.int32, (MXU_TILE, MXU_TILE), 0), HEAD_DIM)
                 == _block_id(lax.broadcasted_iota(jnp.int32, (MXU_TILE, MXU_TILE), 1), HEAD_DIM))
    bd_f32 = jnp.where(same_head, 1.0, 0.0)
    bd_bf = bd_f32.astype(BF16)

    @pl.when(t == 0)
    def _():
        carry_ref[...] = shift0_ref[...]
        for s in range(n_seq):
            for g in range(N_GROUPS):
                c = wkv0_ref[s, g]
                state_ref[s, g] = jnp.concatenate([c] * HEADS_PER_GROUP, axis=1) * bd_f32

    @pl.when(jnp.logical_and(pl.program_id(0) == 0, t == 0))
    def _():
        h_ref[...] = _rms(x_ref[...].reshape(rows, D_MODEL), gpre_ref[...]).astype(BF16)

    h = h_ref[...]

    ps = jnp.dot(h, win_ref[:, 0:SHIFT_W], preferred_element_type=F32)
    puv = jnp.dot(h, win_ref[:, SHIFT_W:SHIFT_W + 2 * GMLP_W], preferred_element_type=F32)

    row_id = lax.broadcasted_iota(jnp.int32, (rows, 1), 0)
    p_prev = pltpu.roll(ps, 1, 0)
    for s in range(n_seq):
        p_prev = jnp.where(row_id == s * seq_rows, carry_ref[s], p_prev)
        last = ps[(s + 1) * seq_rows - 1:(s + 1) * seq_rows, :]
        carry_ref[s] = last
        shifto_ref[s] = last
    xm = ps + (p_prev - ps) * mu_ref[...]
    r = xm[:, 0:RWKV_W]
    k = xm[:, RWKV_W:2 * RWKV_W]
    v = xm[:, 2 * RWKV_W:3 * RWKV_W]
    o = 3 * RWKV_W
    dwa = xm[:, o:o + LORA_W + LORA_A]
    dg = xm[:, o + LORA_W + LORA_A:SHIFT_W]
    lane = lax.broadcasted_iota(jnp.int32, (1, LORA_W + LORA_A), 1)
    dwa = jnp.where(lane < LORA_W, jnp.tanh(dwa), dwa)
    lwa = _dot(dwa, wwa_ref[...])
    zw = w0_ref[...] + lwa[:, 0:RWKV_W]
    lw = -EXP_NEG_HALF * jax.nn.sigmoid(zw)
    a_rate = jax.nn.sigmoid(a0_ref[...] + lwa[:, RWKV_W:2 * RWKV_W])
    g_gate = _dot(jax.nn.sigmoid(dg), wg_ref[...])

    kk = k * kk_ref[...]
    kk = kk * lax.rsqrt(jnp.maximum(_head_sum(kk * kk, bd_bf), 1e-24))
    k_mod = k * (1.0 + (a_rate - 1.0) * ka_ref[...])
    bonus = _head_sum(r * k_mod * rk_ref[...], bd_bf) * v

    ri = lax.broadcasted_iota(jnp.int32, (rows, rows), 0)
    ci = lax.broadcasted_iota(jnp.int32, (rows, rows), 1)
    same_chunk = _block_id(ri, WKV_CHUNK) == _block_id(ci, WKV_CHUNK)
    tri = jnp.where(same_chunk & (ci <= ri), 1.0, 0.0).astype(BF16)
    lw_hi, lw_lo = _split2(lw)
    cs = (jnp.dot(tri, lw_hi, preferred_element_type=F32)
          + jnp.dot(tri, lw_lo, preferred_element_type=F32))
    ends = []
    for c in range(n_chunks):
        cs_end = cs[(c + 1) * WKV_CHUNK - 1:(c + 1) * WKV_CHUNK, :]
        p_end = jnp.exp(cs_end)
        pc_ref[c * 8:(c + 1) * 8, :] = jnp.broadcast_to(p_end, (8, RWKV_W))
        ends.append(jnp.broadcast_to(p_end, (WKV_CHUNK, RWKV_W)))
    p_inv = jnp.exp(-cs)
    to_end = jnp.concatenate(ends, axis=0) * p_inv
    at_ref[...] = (-kk * jnp.exp(cs - lw)).astype(BF16)
    b_vec = kk * a_rate
    bt_ref[...] = (b_vec * p_inv).astype(BF16)
    kt_ref[...] = (k_mod * p_inv).astype(BF16)
    rt_ref[...] = (r * jnp.exp(cs)).astype(BF16)
    vb_ref[...] = v.astype(BF16)
    bh_ref[...] = (b_vec * to_end).astype(BF16)
    kh_ref[...] = (k_mod * to_end).astype(BF16)

    pu = puv[:, 0:GMLP_W]
    pv = puv[:, GMLP_W:2 * GMLP_W]
    mu_v = jnp.mean(pv, axis=-1, keepdims=True)
    pvc = pv - mu_v
    var_v = jnp.mean(pvc * pvc, axis=-1, keepdims=True)
    vn = pvc * lax.rsqrt(var_v + LN_EPS) * lng_ref[...] + lnb_ref[...]
    if emit_vn:
        vno_ref[...] = vn.reshape(vno_ref.shape)
    vn_bf = vn.astype(BF16)
    sp_mask = (_block_id(ri, gmlp_chunk) == _block_id(ci, gmlp_chunk)) & (ci <= ri)
    col = lambda j: slice(j * MXU_TILE, (j + 1) * MXU_TILE)
    zs = [None] * GMLP_GROUPS
    projb_cols = [None] * (D_MODEL // MXU_TILE)
    pg_cols = [None] * (2 * D_MODEL // MXU_TILE)
    ob_box = []

    def spatial(g):
        ws_g = jnp.where(sp_mask, ws_ref[g], 0.0).astype(BF16)
        zs[g] = jnp.dot(ws_g, vn_bf[:, g * GMLP_GC:(g + 1) * GMLP_GC], preferred_element_type=F32)

    def proj_b_col(j):
        if not ob_box:
            z = jnp.concatenate(zs, axis=1) + jnp.concatenate([bias_ref[...]] * (rows // gmlp_chunk), axis=0)
            ob_box.append((pu * z).astype(BF16))
        projb_cols[j] = jnp.dot(ob_box[0], wpb_ref[:, col(j)], preferred_element_type=F32)

    def gate_col(j):
        off = SHIFT_W + 2 * GMLP_W
        pg_cols[j] = jnp.dot(h, win_ref[:, off + j * MXU_TILE:off + (j + 1) * MXU_TILE],
                             preferred_element_type=F32)

    fillers = ([functools.partial(spatial, g) for g in range(GMLP_GROUPS)]
               + [functools.partial(proj_b_col, j) for j in range(len(projb_cols))]
               + [functools.partial(gate_col, j) for j in range(len(pg_cols))])
    fillers.reverse()

    def fill(n=1):
        for _ in range(n):
            if fillers:
                fillers.pop()()

    per_state_stage = -(-len(fillers) // (2 * chunks_per_seq))

    tq = lax.broadcasted_iota(jnp.int32, (WKV_CHUNK, MXU_TILE), 0)
    sq = lax.broadcasted_iota(jnp.int32, (WKV_CHUNK, MXU_TILE), 1) & (WKV_CHUNK - 1)
    strict = sq < tq
    incl = sq <= tq
    eye_all = jnp.where(sq == tq, 1.0, 0.0)

    lane_half = lax.broadcasted_iota(jnp.int32, (WKV_CHUNK, 2 * HEAD_DIM), 1) < HEAD_DIM
    half_masks = [jnp.where(lane_half, 1.0, 0.0).astype(BF16), jnp.where(lane_half, 0.0, 1.0).astype(BF16)]
    zero_tile = jnp.zeros((WKV_CHUNK, 2 * HEAD_DIM), BF16)

    def bdiag(m):
        m = m.astype(BF16)
        blocks = []
        for hd in range(HEADS_PER_GROUP):
            keep = m[:, (hd // 2) * 2 * HEAD_DIM:(hd // 2 + 1) * 2 * HEAD_DIM] * half_masks[hd % 2]
            blocks.append(jnp.concatenate([keep, zero_tile] if hd // 2 == 0 else [zero_tile, keep], axis=1))
        return jnp.concatenate(blocks, axis=0)

    half_f32 = [jnp.where(lane_half, 1.0, 0.0), jnp.where(lane_half, 0.0, 1.0)]
    zero_f32 = jnp.zeros((HEAD_DIM, 2 * HEAD_DIM), F32)

    def decayed_plus(state, decay_row, update):
        blocks = []
        for hd in range(HEADS_PER_GROUP):
            rs = slice(hd * HEAD_DIM, (hd + 1) * HEAD_DIM)
            ls = slice((hd // 2) * 2 * HEAD_DIM, (hd // 2 + 1) * 2 * HEAD_DIM)
            keep = state[rs, ls] * decay_row[:, ls] + update[rs, ls] * half_f32[hd % 2]
            blocks.append(jnp.concatenate([keep, zero_f32] if hd // 2 == 0 else [zero_f32, keep], axis=1))
        return jnp.concatenate(blocks, axis=0)

    groups = range(N_GROUPS)
    lanes = [col(g) for g in groups]
    stack = lambda top_, bottom_: jnp.concatenate([top_.astype(BF16), bottom_.astype(BF16)], axis=0)
    top = lambda m: m[:WKV_CHUNK]
    bottom = lambda m: m[WKV_CHUNK:]
    row_sl = [slice(c * WKV_CHUNK, (c + 1) * WKV_CHUNK) for c in range(n_chunks)]
    chains = [(c, g) for c in range(n_chunks) for g in groups]
    idx = {cg: n for n, cg in enumerate(chains)}
    load = lambda ref: [ref[row_sl[c], lanes[g]] for c, g in chains]
    each = lambda fn, *lists: [fn(*vals) for vals in zip(*lists)]
    at, rt, vv = load(at_ref), load(rt_ref), load(vb_ref)
    lhs = each(stack, at, rt)
    s1 = each(lambda l, m: _dot_nt(l, bdiag(m)), lhs, load(bt_ref))
    s2 = each(lambda l, m: _dot_nt(l, bdiag(m)), lhs, load(kt_ref))
    l_ab = [jnp.where(strict, top(s), 0.0) for s in s1]
    a_rb = [jnp.where(incl, bottom(s), 0.0) for s in s1]
    l_ak = [jnp.where(strict, top(s), 0.0) for s in s2]
    a_rk = [jnp.where(incl, bottom(s), 0.0) for s in s2]
    l_pow = [_dot(l, bdiag(l)) for l in l_ab]
    t_inv = [eye_all + l for l in l_ab]
    for _ in range(_NEUMANN_STEPS):
        out = each(lambda t_, p_: _dot(stack(t_, p_), bdiag(p_)), t_inv, l_pow)
        t_inv = each(lambda t_, o_: t_ + top(o_), t_inv, out)
        l_pow = [bottom(o_) for o_ in out]
    t_inv = each(lambda t_, p_: t_ + _dot(t_, bdiag(p_)), t_inv, l_pow)
    out = each(lambda a_, b_, v_: _dot(stack(a_, b_), bdiag(v_)), l_ak, a_rk, vv)
    x_mid = [top(o_) for o_ in out]
    av = [bottom(o_) for o_ in out]
    tg = each(lambda t_, a_: stack(t_, _dot(a_, bdiag(t_))), t_inv, a_rb)
    out = each(lambda tg_, a_: _dot(tg_, bdiag(a_)), tg, at)
    w_mat = [top(o_) for o_ in out]
    q_hat = each(lambda r__, o_: r__.astype(F32) + bottom(o_), rt, out)
    out = each(lambda tg_, x_: _dot(tg_, bdiag(x_)), tg, x_mid)
    u0 = [top(o_) for o_ in out]
    y0 = each(lambda o_, v_: bottom(o_) + v_, out, av)
    qw = each(stack, q_hat, w_mat)
    bk = each(stack, load(bh_ref), load(kh_ref))
    st = {(s, g): state_ref[s, g] for s in range(n_seq) for g in groups}
    for j in range(chunks_per_seq):
        now = [(s, s * chunks_per_seq + j, g) for s in range(n_seq) for g in groups]
        out = {(s, g): _dot_nt(qw[idx[c, g]], st[s, g]) for s, c, g in now}
        fill(per_state_stage)
        for s, c, g in now:
            y_ref[row_sl[c], lanes[g]] = top(out[s, g]) + y0[idx[c, g]]
        upd = {(s, g): _dot(jnp.concatenate([bottom(out[s, g]) + u0[idx[c, g]], vv[idx[c, g]].astype(F32)],
                                            axis=0).T, bk[idx[c, g]]) for s, c, g in now}
        fill(per_state_stage)
        st = {(s, g): decayed_plus(st[s, g], pc_ref[c * 8:c * 8 + 1, lanes[g]], upd[s, g]) for s, c, g in now}
    for (s, g), val in st.items():
        state_ref[s, g] = val
    fill(len(fillers))

    proj_b = jnp.concatenate(projb_cols, axis=1)
    pg = jnp.concatenate(pg_cols, axis=1)
    halves = [slice(0, rows // 2), slice(rows // 2, rows)]
    inv_hd = 1.0 / HEAD_DIM
    y = [y_ref[hs, :] for hs in halves]
    mean = [_head_sum(y_, bd_bf) * inv_hd for y_ in y]
    yc = [y_ - m_ for y_, m_ in zip(y, mean)]
    var = [_head_sum(yc_ * yc_, bd_bf) * inv_hd for yc_ in yc]
    o_a = [(yc_ * lax.rsqrt(var_ + GN_EPS) * gng_ref[...] + gnb_ref[...] + bonus[hs, :]) * g_gate[hs, :]
           for yc_, var_, hs in zip(yc, var, halves)]
    proj_a = [_dot(o_, wpa_ref[...]) for o_ in o_a]
    m = [jax.nn.sigmoid(pg[hs, 0:D_MODEL]) * pa_ + jax.nn.sigmoid(pg[hs, D_MODEL:2 * D_MODEL]) * proj_b[hs, :]
         for pa_, hs in zip(proj_a, halves)]
    mo = [_dot(m_, wo_ref[...]) for m_ in m]
    x = x_ref[...].reshape(rows, D_MODEL)
    x_out = jnp.concatenate([x[hs, :] + _rms(mo_, gpost_ref[...]) for mo_, hs in zip(mo, halves)], axis=0)
    xo_ref[...] = x_out.reshape(xo_ref.shape)
    h_ref[...] = _rms(xnext_ref[...].reshape(rows, D_MODEL), gpre_ref[...]).astype(BF16)

    @pl.when(t == pl.num_programs(1) - 1)
    def _():
        for s in range(n_seq):
            for g in range(N_GROUPS):
                sg = state_ref[s, g]
                acc = sg[:, 0:HEAD_DIM]
                for j in range(1, HEADS_PER_GROUP):
                    acc = acc + sg[:, j * HEAD_DIM:(j + 1) * HEAD_DIM]
                wkvo_ref[s, g] = acc


def _ffn_kernel(x_ref, gpre_ref, wgate_ref, wup_ref, wdown_ref, gpost_ref, o_ref):
    n = x_ref.shape[0] // FFN_PARTS
    parts = [slice(i * n, (i + 1) * n) for i in range(FFN_PARTS)]
    xs = [x_ref[p, :] for p in parts]
    f = [_rms(x, gpre_ref[...]).astype(BF16) for x in xs]
    gate_up = [(jnp.dot(f_, wgate_ref[...], preferred_element_type=F32),
                jnp.dot(f_, wup_ref[...], preferred_element_type=F32)) for f_ in f]
    hid = [(gate * jax.nn.sigmoid(gate) * up).astype(BF16) for gate, up in gate_up]
    f2 = [jnp.dot(hid_, wdown_ref[...], preferred_element_type=F32) for hid_ in hid]
    for p, x, f2_ in zip(parts, xs, f2):
        o_ref[p, :] = x + _rms(f2_, gpost_ref[...])


def _resident(shape):
    nd = len(shape)
    return pl.BlockSpec(shape, lambda *_: (0,) * nd, pipeline_mode=pl.Buffered(1))


def _mix_call(x, shift0, wkv0_bd, params, *, n_seq, seq_rows, gmlp_chunk, emit_vn):
    batch, seq_len, _ = x.shape
    rows = n_seq * seq_rows
    assert rows == MIX_ROWS and batch % n_seq == 0 and seq_len % seq_rows == 0
    assert seq_rows % gmlp_chunk == 0 and seq_rows % WKV_CHUNK == 0
    grid = (batch // n_seq, seq_len // seq_rows)

    ws = jnp.tile(params['w_spatial'][:, :gmlp_chunk, :gmlp_chunk], (1, rows // gmlp_chunk, rows // gmlp_chunk))
    bias = jnp.repeat(params['b_spatial'][:, :gmlp_chunk].T, GMLP_GC, axis=1)
    weights = [params['g_pre_mix'], params['w_in'], params['mu_shift'], params['w0'], params['w_wa'], params['a0'],
               params['w_lora_g'], params['k_k'], params['k_a'], params['r_k'], params['gn_g'], params['gn_b'],
               params['ln_v_g'], params['ln_v_b'], ws, bias, params['w_proj_a'], params['w_proj_b'], params['w_o'],
               params['g_post_mix']]

    x_spec = pl.BlockSpec((n_seq, seq_rows, D_MODEL), lambda b, t: (b, t, 0))

    def next_tile(b, t):
        flat = jnp.minimum(b * grid[1] + t + 1, grid[0] * grid[1] - 1)
        return flat // grid[1], lax.rem(flat, grid[1]), 0

    xnext_spec = pl.BlockSpec((n_seq, seq_rows, D_MODEL), next_tile)
    per_seq = lambda shape: pl.BlockSpec((n_seq,) + shape, lambda b, t: (b,) + (0,) * len(shape),
                                         pipeline_mode=pl.Buffered(1))
    in_specs = [x_spec, xnext_spec, per_seq((1, SHIFT_W)), per_seq((N_GROUPS, MXU_TILE, HEAD_DIM))]
    in_specs += [_resident(w.shape) for w in weights]
    out_shape = [jax.ShapeDtypeStruct(x.shape, F32),
                 jax.ShapeDtypeStruct((batch, 1, SHIFT_W), F32),
                 jax.ShapeDtypeStruct((batch, N_GROUPS, MXU_TILE, HEAD_DIM), F32)]
    out_specs = [x_spec, per_seq((1, SHIFT_W)), per_seq((N_GROUPS, MXU_TILE, HEAD_DIM))]
    if emit_vn:
        out_shape.append(jax.ShapeDtypeStruct((batch, seq_len, GMLP_W), F32))
        out_specs.append(x_spec)
    act = lambda dt: pltpu.VMEM((rows, RWKV_W), dt)
    scratch = [pltpu.VMEM((n_seq, 1, SHIFT_W), F32),
               pltpu.VMEM((n_seq, N_GROUPS, MXU_TILE, MXU_TILE), F32),
               act(BF16), act(BF16), act(BF16), act(BF16), act(BF16), act(BF16), act(BF16),
               pltpu.VMEM((rows // WKV_CHUNK * 8, RWKV_W), F32), act(F32), pltpu.VMEM((rows, D_MODEL), BF16)]
    kern = functools.partial(_mix_kernel, n_seq=n_seq, seq_rows=seq_rows, gmlp_chunk=gmlp_chunk, emit_vn=emit_vn)
    return pl.pallas_call(
        kern, grid=grid, in_specs=in_specs, out_specs=out_specs, out_shape=out_shape, scratch_shapes=scratch,
        compiler_params=pltpu.CompilerParams(dimension_semantics=("arbitrary", "arbitrary"),
                                             vmem_limit_bytes=VMEM_LIMIT),
    )(x, x, shift0, wkv0_bd, *weights)


def _ffn_call(x2d, params):
    n = x2d.shape[0]
    assert n % FFN_ROWS == 0
    weights = [params['g_pre_ffn'], params['w_gate'], params['w_up'], params['w_down'], params['g_post_ffn']]
    row_spec = pl.BlockSpec((FFN_ROWS, D_MODEL), lambda i: (i, 0))
    return pl.pallas_call(
        _ffn_kernel, grid=(n // FFN_ROWS,), in_specs=[row_spec] + [_resident(w.shape) for w in weights],
        out_specs=row_spec, out_shape=jax.ShapeDtypeStruct(x2d.shape, F32),
        compiler_params=pltpu.CompilerParams(dimension_semantics=("arbitrary",), vmem_limit_bytes=VMEM_LIMIT),
    )(x2d, *weights)


def _layer(x, shift0, wkv0, params, *, n_seq, seq_rows, gmlp_chunk, emit_vn):
    wkv0_g = wkv0.astype(F32).reshape(wkv0.shape[0], N_GROUPS, MXU_TILE, HEAD_DIM)
    outs = _mix_call(x, shift0, wkv0_g, params, n_seq=n_seq, seq_rows=seq_rows,
                     gmlp_chunk=gmlp_chunk, emit_vn=emit_vn)
    x_mid, shift, wkv_g = outs[:3]
    y = _ffn_call(x_mid.reshape(-1, D_MODEL), params).reshape(x.shape)
    return y, shift, wkv_g.reshape(wkv0.shape), (outs[3] if emit_vn else None)


def kernel(x_prompt, x_sample, state_shift, state_wkv, g_pre_mix, w_in, mu_shift, w0, w_lora_w, a0, w_lora_a, w_lora_g, k_k, k_a, r_k, gn_g, gn_b, ln_v_g, ln_v_b, w_spatial, b_spatial, w_proj_a, w_proj_b, w_o, g_post_mix, g_pre_ffn, w_gate, w_up, w_down, g_post_ffn):
    depth = w_in.shape[0]
    yp, ys = x_prompt, x_sample
    bp = x_prompt.shape[0]
    shift_p, wkv_p, shift_s, wkv_s, v_s = [], [], [], [], []
    for l in range(depth):
        row = lambda a: a[l].reshape(1, -1).astype(F32)
        zero = jnp.zeros((LORA_W, RWKV_W), BF16)
        w_wa = jnp.concatenate(
            [jnp.concatenate([w_lora_w[l].astype(BF16), zero], axis=1),
             jnp.concatenate([zero, w_lora_a[l].astype(BF16)], axis=1)], axis=0)
        params = {
            'g_pre_mix': row(g_pre_mix), 'w_in': w_in[l].astype(BF16), 'mu_shift': row(mu_shift), 'w0': row(w0),
            'w_wa': w_wa, 'a0': row(a0), 'w_lora_g': w_lora_g[l].astype(BF16), 'k_k': row(k_k), 'k_a': row(k_a),
            'r_k': row(r_k), 'gn_g': row(gn_g), 'gn_b': row(gn_b), 'ln_v_g': row(ln_v_g), 'ln_v_b': row(ln_v_b),
            'w_spatial': w_spatial[l].astype(F32), 'b_spatial': b_spatial[l].astype(F32),
            'w_proj_a': w_proj_a[l].astype(BF16), 'w_proj_b': w_proj_b[l].astype(BF16), 'w_o': w_o[l].astype(BF16),
            'g_post_mix': row(g_post_mix), 'g_pre_ffn': row(g_pre_ffn), 'w_gate': w_gate[l].astype(BF16),
            'w_up': w_up[l].astype(BF16), 'w_down': w_down[l].astype(BF16), 'g_post_ffn': row(g_post_ffn),
        }
        zero_row = jnp.zeros((bp, 1, SHIFT_W), F32)
        zero_s = jnp.zeros((bp, N_HEADS, HEAD_DIM, HEAD_DIM), F32)
        yp, sp, wp, _ = _layer(yp, zero_row, zero_s, params, n_seq=1, seq_rows=MIX_ROWS,
                               gmlp_chunk=GMLP_CHUNK, emit_vn=False)
        dec_seq = x_sample.shape[1]
        ys, ss, wsm, vs = _layer(ys, state_shift[l], state_wkv[l], params, n_seq=MIX_ROWS // dec_seq,
                                 seq_rows=dec_seq, gmlp_chunk=dec_seq, emit_vn=True)
        shift_p.append(sp); wkv_p.append(wp); shift_s.append(ss); wkv_s.append(wsm); v_s.append(vs)
    return (yp, ys, jnp.stack(shift_p), jnp.stack(wkv_p), jnp.stack(shift_s), jnp.stack(wkv_s), jnp.stack(v_s))
```

```python
import functools

import jax
import jax.numpy as jnp
from jax import lax
from jax.experimental import pallas as pl
from jax.experimental.pallas import tpu as pltpu

F32 = jnp.float32
BF16 = jnp.bfloat16

D_MODEL = 1024
HEAD_DIM = 64
N_HEADS = D_MODEL // HEAD_DIM
RWKV_W = N_HEADS * HEAD_DIM
LORA_W = 64
LORA_A = 64
LORA_G = 128
SHIFT_W = 3 * RWKV_W + LORA_W + LORA_A + LORA_G
GMLP_W = D_MODEL
GMLP_GC = 128
GMLP_CHUNK = 128
GMLP_GROUPS = GMLP_W // GMLP_GC
IN_W = SHIFT_W + 2 * GMLP_W + 2 * D_MODEL
RMS_EPS = 1e-6
GN_EPS = 64e-5
LN_EPS = 1e-5
EXP_NEG_HALF = 0.6065306597126334

MXU_TILE = 256
WKV_CHUNK = 64
HEADS_PER_GROUP = MXU_TILE // HEAD_DIM
N_GROUPS = RWKV_W // MXU_TILE
MIX_ROWS = 256
FFN_ROWS = 1024
FFN_PARTS = 4
VMEM_LIMIT = 60 * 1024 * 1024

_NEUMANN_STEPS = 4


def _dot(a, b):
    return jnp.dot(a.astype(BF16), b.astype(BF16), preferred_element_type=F32)


def _dot_nt(a, b):
    return lax.dot_general(a.astype(BF16), b.astype(BF16), (((1,), (1,)), ((), ())),
                           preferred_element_type=F32)


def _block_id(idx, block):
    assert block & (block - 1) == 0
    return lax.shift_right_logical(idx, block.bit_length() - 1)


def _rms(x, g):
    return x * lax.rsqrt(jnp.mean(x * x, axis=-1, keepdims=True) + RMS_EPS) * g


def _split2(x):
    hi = x.astype(BF16)
    lo = (x - hi.astype(F32)).astype(BF16)
    return hi, lo


def _head_sum(x, ones_bd):
    xb = x.astype(BF16)
    cols = [jnp.dot(xb[:, g * MXU_TILE:(g + 1) * MXU_TILE], ones_bd, preferred_element_type=F32)
            for g in range(N_GROUPS)]
    return jnp.concatenate(cols, axis=1)


def _mix_kernel(x_ref, xnext_ref, shift0_ref, wkv0_ref, gpre_ref, win_ref, mu_ref, w0_ref, wwa_ref, a0_ref, wg_ref,
                kk_ref, ka_ref, rk_ref, gng_ref, gnb_ref, lng_ref, lnb_ref, ws_ref, bias_ref,
                wpa_ref, wpb_ref, wo_ref, gpost_ref,
                xo_ref, shifto_ref, wkvo_ref, *rest, n_seq, seq_rows, gmlp_chunk, emit_vn):
    if emit_vn:
        vno_ref, rest = rest[0], rest[1:]
    (carry_ref, state_ref, at_ref, bt_ref, kt_ref, rt_ref, vb_ref, bh_ref, kh_ref, pc_ref, y_ref, h_ref) = rest
    rows = n_seq * seq_rows
    n_chunks = rows // WKV_CHUNK
    chunks_per_seq = seq_rows // WKV_CHUNK
    t = pl.program_id(1)

    same_head = (_block_id(lax.broadcasted_iota(jnp.int32, (MXU_TILE, MXU_TILE), 0), HEAD_DIM)
                 == _block_id(lax.broadcasted_iota(jnp.int32, (MXU_TILE, MXU_TILE), 1), HEAD_DIM))
    bd_f32 = jnp.where(same_head, 1.0, 0.0)
    bd_bf = bd_f32.astype(BF16)

    @pl.when(t == 0)
    def _():
        carry_ref[...] = shift0_ref[...]
        for s in range(n_seq):
            for g in range(N_GROUPS):
                c = wkv0_ref[s, g]
                state_ref[s, g] = (jnp.concatenate([c] * HEADS_PER_GROUP, axis=1) * bd_f32).T

    @pl.when(jnp.logical_and(pl.program_id(0) == 0, t == 0))
    def _():
        h_ref[...] = _rms(x_ref[...].reshape(rows, D_MODEL), gpre_ref[...]).astype(BF16)

    h = h_ref[...]

    ps = jnp.dot(h, win_ref[:, 0:SHIFT_W], preferred_element_type=F32)
    puv = jnp.dot(h, win_ref[:, SHIFT_W:SHIFT_W + 2 * GMLP_W], preferred_element_type=F32)

    row_id = lax.broadcasted_iota(jnp.int32, (rows, 1), 0)
    p_prev = pltpu.roll(ps, 1, 0)
    for s in range(n_seq):
        p_prev = jnp.where(row_id == s * seq_rows, carry_ref[s], p_prev)
        last = ps[(s + 1) * seq_rows - 1:(s + 1) * seq_rows, :]
        carry_ref[s] = last
        shifto_ref[s] = last
    xm = ps + (p_prev - ps) * mu_ref[...]
    r = xm[:, 0:RWKV_W]
    k = xm[:, RWKV_W:2 * RWKV_W]
    v = xm[:, 2 * RWKV_W:3 * RWKV_W]
    o = 3 * RWKV_W
    dwa = xm[:, o:o + LORA_W + LORA_A]
    dg = xm[:, o + LORA_W + LORA_A:SHIFT_W]
    lane = lax.broadcasted_iota(jnp.int32, (1, LORA_W + LORA_A), 1)
    dwa = jnp.where(lane < LORA_W, jnp.tanh(dwa), dwa)
    lwa = _dot(dwa, wwa_ref[...])
    zw = w0_ref[...] + lwa[:, 0:RWKV_W]
    lw = -EXP_NEG_HALF * jax.nn.sigmoid(zw)
    a_rate = jax.nn.sigmoid(a0_ref[...] + lwa[:, RWKV_W:2 * RWKV_W])
    g_gate = _dot(jax.nn.sigmoid(dg), wg_ref[...])

    kk = k * kk_ref[...]
    kk = kk * lax.rsqrt(jnp.maximum(_head_sum(kk * kk, bd_bf), 1e-24))
    k_mod = k * (1.0 + (a_rate - 1.0) * ka_ref[...])
    bonus = _head_sum(r * k_mod * rk_ref[...], bd_bf) * v

    ri = lax.broadcasted_iota(jnp.int32, (rows, rows), 0)
    ci = lax.broadcasted_iota(jnp.int32, (rows, rows), 1)
    same_chunk = _block_id(ri, WKV_CHUNK) == _block_id(ci, WKV_CHUNK)
    tri = jnp.where(same_chunk & (ci <= ri), 1.0, 0.0).astype(BF16)
    lw_hi, lw_lo = _split2(lw)
    cs = (jnp.dot(tri, lw_hi, preferred_element_type=F32)
          + jnp.dot(tri, lw_lo, preferred_element_type=F32))
    ends = []
    for c in range(n_chunks):
        cs_end = cs[(c + 1) * WKV_CHUNK - 1:(c + 1) * WKV_CHUNK, :]
        p_end = jnp.exp(cs_end)
        pc_ref[c] = jnp.broadcast_to(p_end, (8, RWKV_W)).T
        ends.append(jnp.broadcast_to(p_end, (WKV_CHUNK, RWKV_W)))
    p_inv = jnp.exp(-cs)
    to_end = jnp.concatenate(ends, axis=0) * p_inv
    at_ref[...] = (-kk * jnp.exp(cs - lw)).astype(BF16)
    b_vec = kk * a_rate
    bt_ref[...] = (b_vec * p_inv).astype(BF16)
    kt_ref[...] = (k_mod * p_inv).astype(BF16)
    rt_ref[...] = (r * jnp.exp(cs)).astype(BF16)
    vb_ref[...] = v.astype(BF16)
    bh_ref[...] = (b_vec * to_end).astype(BF16)
    kh_ref[...] = (k_mod * to_end).astype(BF16)

    pu = puv[:, 0:GMLP_W]
    pv = puv[:, GMLP_W:2 * GMLP_W]
    mu_v = jnp.mean(pv, axis=-1, keepdims=True)
    pvc = pv - mu_v
    var_v = jnp.mean(pvc * pvc, axis=-1, keepdims=True)
    vn = pvc * lax.rsqrt(var_v + LN_EPS) * lng_ref[...] + lnb_ref[...]
    if emit_vn:
        vno_ref[...] = vn.reshape(vno_ref.shape)
    vn_bf = vn.astype(BF16)
    sp_mask = (_block_id(ri, gmlp_chunk) == _block_id(ci, gmlp_chunk)) & (ci <= ri)
    col = lambda j: slice(j * MXU_TILE, (j + 1) * MXU_TILE)
    zs = [None] * GMLP_GROUPS
    projb_cols = [None] * (D_MODEL // MXU_TILE)
    pg_cols = [None] * (2 * D_MODEL // MXU_TILE)
    ob_box = []

    def spatial(g):
        ws_g = jnp.where(sp_mask, ws_ref[g], 0.0).astype(BF16)
        zs[g] = jnp.dot(ws_g, vn_bf[:, g * GMLP_GC:(g + 1) * GMLP_GC], preferred_element_type=F32)

    def proj_b_col(j):
        if not ob_box:
            z = jnp.concatenate(zs, axis=1) + jnp.concatenate([bias_ref[...]] * (rows // gmlp_chunk), axis=0)
            ob_box.append((pu * z).astype(BF16))
        projb_cols[j] = jnp.dot(ob_box[0], wpb_ref[:, col(j)], preferred_element_type=F32)

    def gate_col(j):
        off = SHIFT_W + 2 * GMLP_W
        pg_cols[j] = jnp.dot(h, win_ref[:, off + j * MXU_TILE:off + (j + 1) * MXU_TILE],
                             preferred_element_type=F32)

    fillers = ([functools.partial(spatial, g) for g in range(GMLP_GROUPS)]
               + [functools.partial(proj_b_col, j) for j in range(len(projb_cols))]
               + [functools.partial(gate_col, j) for j in range(len(pg_cols))])
    fillers.reverse()

    def fill(n=1):
        for _ in range(n):
            if fillers:
                fillers.pop()()

    per_state_stage = -(-len(fillers) // (2 * chunks_per_seq))

    tq = lax.broadcasted_iota(jnp.int32, (WKV_CHUNK, MXU_TILE), 0)
    sq = lax.broadcasted_iota(jnp.int32, (WKV_CHUNK, MXU_TILE), 1) & (WKV_CHUNK - 1)
    strict = sq < tq
    incl = sq <= tq
    eye_all = jnp.where(sq == tq, 1.0, 0.0)

    def bdiag(m):
        m = m.astype(BF16)
        return jnp.concatenate([m] * HEADS_PER_GROUP, axis=0) * bd_bf

    groups = range(N_GROUPS)
    lanes = [col(g) for g in groups]
    stack = lambda top_, bottom_: jnp.concatenate([top_.astype(BF16), bottom_.astype(BF16)], axis=0)
    top = lambda m: m[:WKV_CHUNK]
    bottom = lambda m: m[WKV_CHUNK:]
    row_sl = [slice(c * WKV_CHUNK, (c + 1) * WKV_CHUNK) for c in range(n_chunks)]
    chains = [(c, g) for c in range(n_chunks) for g in groups]
    idx = {cg: n for n, cg in enumerate(chains)}
    load = lambda ref: [ref[row_sl[c], lanes[g]] for c, g in chains]
    each = lambda fn, *lists: [fn(*vals) for vals in zip(*lists)]
    at, rt, vv = load(at_ref), load(rt_ref), load(vb_ref)
    lhs = each(stack, at, rt)
    s1 = each(lambda l, m: _dot_nt(l, bdiag(m)), lhs, load(bt_ref))
    s2 = each(lambda l, m: _dot_nt(l, bdiag(m)), lhs, load(kt_ref))
    l_ab = [jnp.where(strict, top(s), 0.0) for s in s1]
    a_rb = [jnp.where(incl, bottom(s), 0.0) for s in s1]
    l_ak = [jnp.where(strict, top(s), 0.0) for s in s2]
    a_rk = [jnp.where(incl, bottom(s), 0.0) for s in s2]
    l_pow = [_dot(l, bdiag(l)) for l in l_ab]
    t_inv = [eye_all + l for l in l_ab]
    for _ in range(_NEUMANN_STEPS):
        out = each(lambda t_, p_: _dot(stack(t_, p_), bdiag(p_)), t_inv, l_pow)
        t_inv = each(lambda t_, o_: t_ + top(o_), t_inv, out)
        l_pow = [bottom(o_) for o_ in out]
    t_inv = each(lambda t_, p_: t_ + _dot(t_, bdiag(p_)), t_inv, l_pow)
    out = each(lambda a_, b_, v_: _dot(stack(a_, b_), bdiag(v_)), l_ak, a_rk, vv)
    x_mid = [top(o_) for o_ in out]
    av = [bottom(o_) for o_ in out]
    tg = each(lambda t_, a_: stack(t_, _dot(a_, bdiag(t_))), t_inv, a_rb)
    out = each(lambda tg_, a_: _dot(tg_, bdiag(a_)), tg, at)
    w_mat = [top(o_) for o_ in out]
    q_hat = each(lambda r__, o_: r__.astype(F32) + bottom(o_), rt, out)
    out = each(lambda tg_, x_: _dot(tg_, bdiag(x_)), tg, x_mid)
    u0 = [top(o_) for o_ in out]
    y0 = each(lambda o_, v_: bottom(o_) + v_, out, av)
    qw = each(stack, q_hat, w_mat)
    bk = each(stack, load(bh_ref), load(kh_ref))
    st = {(s, g): state_ref[s, g] for s in range(n_seq) for g in groups}
    for j in range(chunks_per_seq):
        now = [(s, s * chunks_per_seq + j, g) for s in range(n_seq) for g in groups]
        out = {(s, g): _dot(qw[idx[c, g]], st[s, g]) for s, c, g in now}
        fill(per_state_stage)
        for s, c, g in now:
            y_ref[row_sl[c], lanes[g]] = top(out[s, g]) + y0[idx[c, g]]
        upd = {(s, g): _dot(bk[idx[c, g]].astype(F32).T,
                            jnp.concatenate([bottom(out[s, g]) + u0[idx[c, g]], vv[idx[c, g]].astype(F32)], axis=0))
               for s, c, g in now}
        fill(per_state_stage)
        st = {(s, g): st[s, g] * pc_ref[c, lanes[g], 0:1] + upd[s, g] * bd_f32 for s, c, g in now}
    for (s, g), val in st.items():
        state_ref[s, g] = val
    fill(len(fillers))

    proj_b = jnp.concatenate(projb_cols, axis=1)
    pg = jnp.concatenate(pg_cols, axis=1)
    halves = [slice(0, rows // 2), slice(rows // 2, rows)]
    inv_hd = 1.0 / HEAD_DIM
    y = [y_ref[hs, :] for hs in halves]
    mean = [_head_sum(y_, bd_bf) * inv_hd for y_ in y]
    yc = [y_ - m_ for y_, m_ in zip(y, mean)]
    var = [_head_sum(yc_ * yc_, bd_bf) * inv_hd for yc_ in yc]
    o_a = [(yc_ * lax.rsqrt(var_ + GN_EPS) * gng_ref[...] + gnb_ref[...] + bonus[hs, :]) * g_gate[hs, :]
           for yc_, var_, hs in zip(yc, var, halves)]
    proj_a = [_dot(o_, wpa_ref[...]) for o_ in o_a]
    m = [jax.nn.sigmoid(pg[hs, 0:D_MODEL]) * pa_ + jax.nn.sigmoid(pg[hs, D_MODEL:2 * D_MODEL]) * proj_b[hs, :]
         for pa_, hs in zip(proj_a, halves)]
    mo = [_dot(m_, wo_ref[...]) for m_ in m]
    x = x_ref[...].reshape(rows, D_MODEL)
    x_out = jnp.concatenate([x[hs, :] + _rms(mo_, gpost_ref[...]) for mo_, hs in zip(mo, halves)], axis=0)
    xo_ref[...] = x_out.reshape(xo_ref.shape)
    h_ref[...] = _rms(xnext_ref[...].reshape(rows, D_MODEL), gpre_ref[...]).astype(BF16)

    @pl.when(t == pl.num_programs(1) - 1)
    def _():
        for s in range(n_seq):
            for g in range(N_GROUPS):
                sg = state_ref[s, g].T
                acc = sg[:, 0:HEAD_DIM]
                for j in range(1, HEADS_PER_GROUP):
                    acc = acc + sg[:, j * HEAD_DIM:(j + 1) * HEAD_DIM]
                wkvo_ref[s, g] = acc


def _ffn_kernel(x_ref, gpre_ref, wgate_ref, wup_ref, wdown_ref, gpost_ref, o_ref):
    n = x_ref.shape[0] // FFN_PARTS
    parts = [slice(i * n, (i + 1) * n) for i in range(FFN_PARTS)]
    xs = [x_ref[p, :] for p in parts]
    f = [_rms(x, gpre_ref[...]).astype(BF16) for x in xs]
    gate_up = [(jnp.dot(f_, wgate_ref[...], preferred_element_type=F32),
                jnp.dot(f_, wup_ref[...], preferred_element_type=F32)) for f_ in f]
    hid = [(gate * jax.nn.sigmoid(gate) * up).astype(BF16) for gate, up in gate_up]
    f2 = [jnp.dot(hid_, wdown_ref[...], preferred_element_type=F32) for hid_ in hid]
    for p, x, f2_ in zip(parts, xs, f2):
        o_ref[p, :] = x + _rms(f2_, gpost_ref[...])


def _resident(shape):
    nd = len(shape)
    return pl.BlockSpec(shape, lambda *_: (0,) * nd, pipeline_mode=pl.Buffered(1))


def _mix_call(x, shift0, wkv0_bd, params, *, n_seq, seq_rows, gmlp_chunk, emit_vn):
    batch, seq_len, _ = x.shape
    rows = n_seq * seq_rows
    assert rows == MIX_ROWS and batch % n_seq == 0 and seq_len % seq_rows == 0
    assert seq_rows % gmlp_chunk == 0 and seq_rows % WKV_CHUNK == 0
    grid = (batch // n_seq, seq_len // seq_rows)

    ws = jnp.tile(params['w_spatial'][:, :gmlp_chunk, :gmlp_chunk], (1, rows // gmlp_chunk, rows // gmlp_chunk))
    bias = jnp.repeat(params['b_spatial'][:, :gmlp_chunk].T, GMLP_GC, axis=1)
    weights = [params['g_pre_mix'], params['w_in'], params['mu_shift'], params['w0'], params['w_wa'], params['a0'],
               params['w_lora_g'], params['k_k'], params['k_a'], params['r_k'], params['gn_g'], params['gn_b'],
               params['ln_v_g'], params['ln_v_b'], ws, bias, params['w_proj_a'], params['w_proj_b'], params['w_o'],
               params['g_post_mix']]

    x_spec = pl.BlockSpec((n_seq, seq_rows, D_MODEL), lambda b, t: (b, t, 0))

    def next_tile(b, t):
        flat = jnp.minimum(b * grid[1] + t + 1, grid[0] * grid[1] - 1)
        return flat // grid[1], lax.rem(flat, grid[1]), 0

    xnext_spec = pl.BlockSpec((n_seq, seq_rows, D_MODEL), next_tile)
    per_seq = lambda shape: pl.BlockSpec((n_seq,) + shape, lambda b, t: (b,) + (0,) * len(shape),
                                         pipeline_mode=pl.Buffered(1))
    in_specs = [x_spec, xnext_spec, per_seq((1, SHIFT_W)), per_seq((N_GROUPS, MXU_TILE, HEAD_DIM))]
    in_specs += [_resident(w.shape) for w in weights]
    out_shape = [jax.ShapeDtypeStruct(x.shape, F32),
                 jax.ShapeDtypeStruct((batch, 1, SHIFT_W), F32),
                 jax.ShapeDtypeStruct((batch, N_GROUPS, MXU_TILE, HEAD_DIM), F32)]
    out_specs = [x_spec, per_seq((1, SHIFT_W)), per_seq((N_GROUPS, MXU_TILE, HEAD_DIM))]
    if emit_vn:
        out_shape.append(jax.ShapeDtypeStruct((batch, seq_len, GMLP_W), F32))
        out_specs.append(x_spec)
    act = lambda dt: pltpu.VMEM((rows, RWKV_W), dt)
    scratch = [pltpu.VMEM((n_seq, 1, SHIFT_W), F32),
               pltpu.VMEM((n_seq, N_GROUPS, MXU_TILE, MXU_TILE), F32),
               act(BF16), act(BF16), act(BF16), act(BF16), act(BF16), act(BF16), act(BF16),
               pltpu.VMEM((rows // WKV_CHUNK, RWKV_W, 8), F32), act(F32), pltpu.VMEM((rows, D_MODEL), BF16)]
    kern = functools.partial(_mix_kernel, n_seq=n_seq, seq_rows=seq_rows, gmlp_chunk=gmlp_chunk, emit_vn=emit_vn)
    return pl.pallas_call(
        kern, grid=grid, in_specs=in_specs, out_specs=out_specs, out_shape=out_shape, scratch_shapes=scratch,
        compiler_params=pltpu.CompilerParams(dimension_semantics=("arbitrary", "arbitrary"),
                                             vmem_limit_bytes=VMEM_LIMIT),
    )(x, x, shift0, wkv0_bd, *weights)


def _ffn_call(x2d, params):
    n = x2d.shape[0]
    assert n % FFN_ROWS == 0
    weights = [params['g_pre_ffn'], params['w_gate'], params['w_up'], params['w_down'], params['g_post_ffn']]
    row_spec = pl.BlockSpec((FFN_ROWS, D_MODEL), lambda i: (i, 0))
    return pl.pallas_call(
        _ffn_kernel, grid=(n // FFN_ROWS,), in_specs=[row_spec] + [_resident(w.shape) for w in weights],
        out_specs=row_spec, out_shape=jax.ShapeDtypeStruct(x2d.shape, F32),
        compiler_params=pltpu.CompilerParams(dimension_semantics=("arbitrary",), vmem_limit_bytes=VMEM_LIMIT),
    )(x2d, *weights)


def _layer(x, shift0, wkv0, params, *, n_seq, seq_rows, gmlp_chunk, emit_vn):
    wkv0_g = wkv0.astype(F32).reshape(wkv0.shape[0], N_GROUPS, MXU_TILE, HEAD_DIM)
    outs = _mix_call(x, shift0, wkv0_g, params, n_seq=n_seq, seq_rows=seq_rows,
                     gmlp_chunk=gmlp_chunk, emit_vn=emit_vn)
    x_mid, shift, wkv_g = outs[:3]
    y = _ffn_call(x_mid.reshape(-1, D_MODEL), params).reshape(x.shape)
    return y, shift, wkv_g.reshape(wkv0.shape), (outs[3] if emit_vn else None)


def kernel(x_prompt, x_sample, state_shift, state_wkv, g_pre_mix, w_in, mu_shift, w0, w_lora_w, a0, w_lora_a, w_lora_g, k_k, k_a, r_k, gn_g, gn_b, ln_v_g, ln_v_b, w_spatial, b_spatial, w_proj_a, w_proj_b, w_o, g_post_mix, g_pre_ffn, w_gate, w_up, w_down, g_post_ffn):
    depth = w_in.shape[0]
    yp, ys = x_prompt, x_sample
    bp = x_prompt.shape[0]
    shift_p, wkv_p, shift_s, wkv_s, v_s = [], [], [], [], []
    for l in range(depth):
        row = lambda a: a[l].reshape(1, -1).astype(F32)
        zero = jnp.zeros((LORA_W, RWKV_W), BF16)
        w_wa = jnp.concatenate(
            [jnp.concatenate([w_lora_w[l].astype(BF16), zero], axis=1),
             jnp.concatenate([zero, w_lora_a[l].astype(BF16)], axis=1)], axis=0)
        params = {
            'g_pre_mix': row(g_pre_mix), 'w_in': w_in[l].astype(BF16), 'mu_shift': row(mu_shift), 'w0': row(w0),
            'w_wa': w_wa, 'a0': row(a0), 'w_lora_g': w_lora_g[l].astype(BF16), 'k_k': row(k_k), 'k_a': row(k_a),
            'r_k': row(r_k), 'gn_g': row(gn_g), 'gn_b': row(gn_b), 'ln_v_g': row(ln_v_g), 'ln_v_b': row(ln_v_b),
            'w_spatial': w_spatial[l].astype(F32), 'b_spatial': b_spatial[l].astype(F32),
            'w_proj_a': w_proj_a[l].astype(BF16), 'w_proj_b': w_proj_b[l].astype(BF16), 'w_o': w_o[l].astype(BF16),
            'g_post_mix': row(g_post_mix), 'g_pre_ffn': row(g_pre_ffn), 'w_gate': w_gate[l].astype(BF16),
            'w_up': w_up[l].astype(BF16), 'w_down': w_down[l].astype(BF16), 'g_post_ffn': row(g_post_ffn),
        }
        zero_row = jnp.zeros((bp, 1, SHIFT_W), F32)
        zero_s = jnp.zeros((bp, N_HEADS, HEAD_DIM, HEAD_DIM), F32)
        yp, sp, wp, _ = _layer(yp, zero_row, zero_s, params, n_seq=1, seq_rows=MIX_ROWS,
                               gmlp_chunk=GMLP_CHUNK, emit_vn=False)
        dec_seq = x_sample.shape[1]
        ys, ss, wsm, vs = _layer(ys, state_shift[l], state_wkv[l], params, n_seq=MIX_ROWS // dec_seq,
                                 seq_rows=dec_seq, gmlp_chunk=dec_seq, emit_vn=True)
        shift_p.append(sp); wkv_p.append(wp); shift_s.append(ss); wkv_s.append(wsm); v_s.append(vs)
    return (yp, ys, jnp.stack(shift_p), jnp.stack(wkv_p), jnp.stack(shift_s), jnp.stack(wkv_s), jnp.stack(v_s))
```
